```python
import math
import jax, jax.numpy as jnp
from jax import lax
import numpy as np

D_MODEL = 2048
BATCH = 16
SEQ = 256
DEPTH = 2
DEC_BATCH = 8
DEC_SEQ = 2048
PAST_LEN = 512

GRID_W = 64
N_GROUPS = 4
G_W = D_MODEL // N_GROUPS
D_MIX = N_GROUPS * G_W
D_FF = ((8 * D_MODEL + 3 * 256 - 1) // (3 * 256)) * 256
EPS = 1e-6
A_CHUNK = 128
A_H = 4
A_HD = G_W // A_H
HG_H = 4
HG_DK = G_W // HG_H
HG_DV = G_W // HG_H
HG_CHUNK = 64
LRU_H = 8
LRU_HD = G_W // LRU_H
LRU_C = 8.0
CONV_K = 4
CONV_LEFT = 2
SSD_H = 8
SSD_P = G_W // SSD_H
SSD_G = 2
SSD_N = 128
SSD_CHUNK = 128
SSD_XBC = G_W + 2 * SSD_G * SSD_N
IN_SPLITS = (G_W, G_W,
             G_W, G_W, G_W, G_W, G_W,
             G_W, G_W,
             G_W, G_W, SSD_G * SSD_N, SSD_G * SSD_N, SSD_H, SSD_H)
D_IN = sum(IN_SPLITS)

kernel_name = "hybrid_diffusion_parallel_heads_step"


def rmsnorm(x, g):
    xf = x.astype(jnp.float32)
    y = xf * lax.rsqrt(jnp.mean(xf * xf, axis=-1, keepdims=True) + EPS)
    return (y * g.astype(jnp.float32)).astype(x.dtype)


def flip(t):
    return jnp.flip(t, axis=1)


def conv_centered(x, w, b, rows):
    bsz, length, ch = x.shape
    xs = x[:, None] if rows is None else x.reshape(bsz, rows, length // rows, ch)
    width = xs.shape[2]
    xp = jnp.pad(xs, ((0, 0), (0, 0), (CONV_LEFT, CONV_K - 1 - CONV_LEFT), (0, 0)))
    y = xp[:, :, 0:width] * w[0] + b
    for j in range(1, CONV_K):
        y = y + xp[:, :, j:j + width] * w[j]
    return y.reshape(bsz, length, ch)


def segsum(a):
    t = a.shape[-1]
    cs = jnp.cumsum(a, axis=-1)
    d = cs[..., :, None] - cs[..., None, :]
    return jnp.where(jnp.tril(jnp.ones((t, t), dtype=bool)), d, -jnp.inf)


def hgrn_scan(q, k, v, log_f, s0):
    bsz, length, h, _ = q.shape
    vdim = v.shape[-1]
    n = length // HG_CHUNK

    def chunks(t):
        return t.reshape(bsz, n, HG_CHUNK, h, t.shape[-1]).transpose(1, 0, 3, 2, 4)

    causal = jnp.tril(jnp.ones((HG_CHUNK, HG_CHUNK), dtype=bool))[None, None, :, :, None]

    def step(s, inp):
        qc, kc, vc, gc = inp
        b = jnp.cumsum(gc, axis=2)
        rel = jnp.where(causal, b[:, :, :, None, :] - b[:, :, None, :, :], -jnp.inf)
        scores = jnp.sum(qc[:, :, :, None, :] * kc[:, :, None, :, :] * jnp.exp(rel), axis=-1)
        o = (jnp.einsum("bhtk,bhkv->bhtv", qc * jnp.exp(b), s)
             + jnp.einsum("bhts,bhsv->bhtv", scores, vc))
        b_end = b[:, :, -1:, :]
        s = (jnp.exp(b_end[:, :, 0, :, None]) * s
             + jnp.einsum("bhsk,bhsv->bhkv", kc * jnp.exp(b_end - b), vc))
        return s, o

    s, o = lax.scan(step, s0.astype(jnp.float32), (chunks(q), chunks(k), chunks(v), chunks(log_f)))
    return o.transpose(1, 0, 3, 2, 4).reshape(bsz, length, h, vdim), s


def linear_scan(a, b, h0):
    b = b.at[:, 0].add(a[:, 0] * h0)

    def combine(e1, e2):
        return (e1[0] * e2[0], e2[0] * e1[1] + e2[1])

    _, h = lax.associative_scan(combine, (a, b), axis=1)
    return h


def ssd_scan(x, dt, bm, cm, a_log, h0):
    bsz, length, h, p = x.shape
    n = length // SSD_CHUNK
    a = (-jnp.exp(a_log) * dt).reshape(bsz, n, SSD_CHUNK, h).transpose(0, 3, 1, 2)
    xd = (x * dt[..., None]).reshape(bsz, n, SSD_CHUNK, h, p)
    bc = bm.reshape(bsz, n, SSD_CHUNK, h, SSD_N)
    cc = cm.reshape(bsz, n, SSD_CHUNK, h, SSD_N)
    a_cum = jnp.cumsum(a, axis=-1)
    scores = jnp.einsum("bclhn,bcshn->bhcls", cc, bc) * jnp.exp(segsum(a))
    y_diag = jnp.einsum("bhcls,bcshp->bclhp", scores, xd)
    decay_to_end = jnp.exp(a_cum[..., -1:] - a_cum)
    chunk_states = jnp.einsum("bclhn,bhcl,bclhp->bchpn", bc, decay_to_end, xd)
    chunk_states = jnp.concatenate([h0[:, None], chunk_states], axis=1)
    chunk_decay = jnp.exp(segsum(jnp.pad(a_cum[..., -1], ((0, 0), (0, 0), (1, 0)))))
    states = jnp.einsum("bhzc,bchpn->bzhpn", chunk_decay, chunk_states)
    y_off = jnp.einsum("bclhn,bchpn,bhcl->bclhp", cc, states[:, :-1], jnp.exp(a_cum))
    return (y_diag + y_off).reshape(bsz, length, h, p), states[:, -1]


def token_mixers(h, p, l, rows, st_hg, st_lru, st_ssd):
    f32 = jnp.float32
    bsz, length, _ = h.shape
    proj = h @ p["w_in"][l]
    cuts = np.cumsum(IN_SPLITS)[:-1].tolist()
    (a_u, a_v, b_q, b_ff, b_fb, b_i, b_g, c_x, c_g,
     d_z, d_x, d_b, d_c, d_dtf, d_dtb) = jnp.split(proj, cuts, axis=-1)

    def heads(t, hd):
        return t.reshape(bsz, length, -1, hd)

    u = jax.nn.gelu(a_u)
    v = rmsnorm(jax.nn.gelu(a_v), p["gmlp_norm_g"][l]).reshape(bsz, length // A_CHUNK, A_CHUNK, A_H, A_HD)
    v = jnp.einsum("hts,bnshd->bnthd", p["gmlp_ws"][l], v) + p["gmlp_bs"][l].T[:, :, None]
    y_a = u * v.reshape(bsz, length, G_W)

    lbs = jax.nn.softmax(p["hgrn_lb"].astype(f32), axis=0)
    lb = (jnp.cumsum(lbs, axis=0) - lbs[0])[l]
    q = heads(jax.nn.silu(b_q).astype(f32), HG_DK)
    iv = heads(b_i.astype(f32), HG_DV)

    def hgrn_dir(logits, s0, rev):
        f = lb + (1.0 - lb) * jax.nn.sigmoid(logits.astype(f32))
        args = (q, heads(1.0 - f, HG_DK), iv, heads(jnp.log(f), HG_DK))
        if rev:
            args = tuple(flip(t) for t in args)
        o, s = hgrn_scan(*args, s0)
        return (flip(o) if rev else o), s

    o_f, s_hf = hgrn_dir(b_ff, st_hg[:, 0], False)
    o_b, s_hb = hgrn_dir(b_fb, st_hg[:, 1], True)
    y_b = (rmsnorm(o_f + o_b, p["hgrn_norm_g"][l])
           * jax.nn.silu(heads(b_g, HG_DV)).astype(f32)).reshape(bsz, length, G_W)

    xc = conv_centered(c_x, p["lru_conv_w"][l], p["lru_conv_b"][l], rows).astype(f32)
    xch = xc.reshape(bsz, length, LRU_H, LRU_HD)

    def lru_dir(d, h0, rev):
        r = jax.nn.sigmoid(jnp.einsum("blhi,hij->blhj", xch, p["lru_wr"][l, d].astype(f32)).reshape(bsz, length, G_W)
                           + p["lru_br"][l, d])
        ig = jax.nn.sigmoid(jnp.einsum("blhi,hij->blhj", xch, p["lru_wi"][l, d].astype(f32)).reshape(bsz, length, G_W)
                            + p["lru_bi"][l, d])
        log_a = LRU_C * r * jax.nn.log_sigmoid(p["lru_lambda"][l, d].astype(f32))
        a = jnp.exp(log_a)
        bx = jnp.sqrt(-jnp.expm1(2.0 * log_a)) * (ig * xc)
        if rev:
            a, bx = flip(a), flip(bx)
        hs = linear_scan(a, bx, h0.astype(f32))
        return (flip(hs) if rev else hs), hs[:, -1]

    h_f, s_lf = lru_dir(0, st_lru[:, 0], False)
    h_b, s_lb = lru_dir(1, st_lru[:, 1], True)
    y_c = (h_f + h_b) * jax.nn.gelu(c_g).astype(f32)

    xbc = jax.nn.silu(conv_centered(jnp.concatenate([d_x, d_b, d_c], axis=-1),
                                    p["ssd_conv_w"][l], p["ssd_conv_b"][l], rows)).astype(f32)
    xs, bm, cm = jnp.split(xbc, [G_W, G_W + SSD_G * SSD_N], axis=-1)
    xs = xs.reshape(bsz, length, SSD_H, SSD_P)
    rep = SSD_H // SSD_G
    bm = jnp.repeat(bm.reshape(bsz, length, SSD_G, SSD_N), rep, axis=2)
    cm = jnp.repeat(cm.reshape(bsz, length, SSD_G, SSD_N), rep, axis=2)

    def ssd_dir(dt_raw, d, h0, rev):
        dt = jax.nn.softplus(dt_raw.astype(f32) + p["ssd_dt_bias"][l, d])
        args = (xs, dt, bm, cm)
        if rev:
            args = tuple(flip(t) for t in args)
        y, s = ssd_scan(*args, p["ssd_a_log"][l, d].astype(f32), h0.astype(f32))
        return (flip(y) if rev else y), s

    y_f, s_sf = ssd_dir(d_dtf, 0, st_ssd[:, 0], False)
    y_r, s_sb = ssd_dir(d_dtb, 1, st_ssd[:, 1], True)
    y_d = (y_f + y_r + p["ssd_d"][l][:, None] * xs).reshape(bsz, length, G_W) * jax.nn.silu(d_z).astype(f32)
    y_d = rmsnorm(y_d.reshape(bsz, length, SSD_G, G_W // SSD_G),
                  p["ssd_norm_g"][l].reshape(SSD_G, G_W // SSD_G)).reshape(bsz, length, G_W)

    y = jnp.concatenate([y_a.astype(f32), y_b, y_c, y_d], axis=-1).astype(h.dtype) @ p["w_out"][l]
    return (y, jnp.stack([s_hf, s_hb], axis=1), jnp.stack([s_lf, s_lb], axis=1),
            jnp.stack([s_sf, s_sb], axis=1))


def trunk_layer(x, cond, p, l, rows, st_hg, st_lru, st_ssd):
    mod = (jax.nn.silu(cond) @ p["w_mod"][l] + p["b_mod"][l])[:, None, :]
    sh1, sc1, g1, sh2, sc2, g2 = jnp.split(mod, 6, axis=-1)
    h = rmsnorm(x, p["norm1_g"][l]) * (1.0 + sc1) + sh1
    y, s_hg, s_lru, s_ssd = token_mixers(h, p, l, rows, st_hg, st_lru, st_ssd)
    x = x + g1 * y
    h = rmsnorm(x, p["norm2_g"][l]) * (1.0 + sc2) + sh2
    ff = (jax.nn.silu(h @ p["ffn_w1"][l]) * (h @ p["ffn_w3"][l])) @ p["ffn_w2"][l]
    x = x + g2 * ff
    return x, s_hg, s_lru, s_ssd


def setup_inputs(seed: int = 0) -> dict:
    key = jax.random.key(seed)
    ks = iter(list(jax.random.split(key, 40)))
    f32 = jnp.float32

    def nrm(shape, scale):
        return scale * jax.random.normal(next(ks), shape, f32)

    def gain(shape):
        return 1.0 + 0.01 * jax.random.normal(next(ks), shape, f32)

    def unif(shape, lo, hi):
        return jax.random.uniform(next(ks), shape, f32, lo, hi)

    a8 = unif((DEPTH, 2, G_W), 0.9, 0.999)
    s = a8 ** (1.0 / LRU_C)
    lru_lambda = jnp.log(s) - jnp.log1p(-s)
    dt0 = jnp.exp(unif((DEPTH, 2, SSD_H), math.log(1e-3), math.log(1e-1)))
    ssd_dt_bias = dt0 + jnp.log(-jnp.expm1(-dt0))
    ssd_a_log = jnp.log(unif((DEPTH, 2, SSD_H), 1.0, 16.0))
    return {
        "x_prompt": nrm((BATCH, SEQ, D_MODEL), 1.0),
        "x_sample": nrm((DEC_BATCH, DEC_SEQ, D_MODEL), 1.0),
        "state_hgrn": nrm((DEC_BATCH, DEPTH, 2, HG_H, HG_DK, HG_DV), 0.5),
        "state_rglru": nrm((DEC_BATCH, DEPTH, 2, G_W), 0.5),
        "state_ssd": nrm((DEC_BATCH, DEPTH, 2, SSD_H, SSD_P, SSD_N), 0.5),
        "c": nrm((DEC_BATCH, D_MODEL), 1.0),
        "c_ctx": nrm((D_MODEL,), 1.0),
        "w_mod": nrm((DEPTH, D_MODEL, 6 * D_MODEL), 0.5 * D_MODEL ** -0.5),
        "b_mod": nrm((DEPTH, 6 * D_MODEL), 0.01),
        "norm1_g": gain((DEPTH, D_MODEL)),
        "norm2_g": gain((DEPTH, D_MODEL)),
        "w_in": nrm((DEPTH, D_MODEL, D_IN), D_MODEL ** -0.5),
        "w_out": nrm((DEPTH, D_MIX, D_MODEL), D_MIX ** -0.5),
        "gmlp_norm_g": gain((DEPTH, G_W)),
        "gmlp_ws": nrm((DEPTH, A_H, A_CHUNK, A_CHUNK), A_CHUNK ** -0.5),
        "gmlp_bs": gain((DEPTH, A_H, A_CHUNK)),
        "hgrn_lb": nrm((DEPTH, G_W), 1.0),
        "hgrn_norm_g": gain((DEPTH, HG_DV)),
        "lru_conv_w": nrm((DEPTH, CONV_K, G_W), 0.5 * CONV_K ** -0.5),
        "lru_conv_b": nrm((DEPTH, G_W), 0.01),
        "lru_wr": nrm((DEPTH, 2, LRU_H, LRU_HD, LRU_HD), LRU_HD ** -0.5),
        "lru_br": nrm((DEPTH, 2, G_W), 0.1),
        "lru_wi": nrm((DEPTH, 2, LRU_H, LRU_HD, LRU_HD), LRU_HD ** -0.5),
        "lru_bi": nrm((DEPTH, 2, G_W), 0.1),
        "lru_lambda": lru_lambda,
        "ssd_conv_w": nrm((DEPTH, CONV_K, SSD_XBC), 0.5 * CONV_K ** -0.5),
        "ssd_conv_b": nrm((DEPTH, SSD_XBC), 0.01),
        "ssd_dt_bias": ssd_dt_bias,
        "ssd_a_log": ssd_a_log,
        "ssd_d": gain((DEPTH, SSD_H)),
        "ssd_norm_g": gain((DEPTH, G_W)),
        "ffn_w1": nrm((DEPTH, D_MODEL, D_FF), D_MODEL ** -0.5),
        "ffn_w3": nrm((DEPTH, D_MODEL, D_FF), D_MODEL ** -0.5),
        "ffn_w2": nrm((DEPTH, D_FF, D_MODEL), D_FF ** -0.5),
        "final_norm_g": gain((D_MODEL,)),
    }


def reference(x_prompt, x_sample, state_hgrn, state_rglru, state_ssd, c, c_ctx, w_mod, b_mod, norm1_g, norm2_g,
              w_in, w_out, gmlp_norm_g, gmlp_ws, gmlp_bs, hgrn_lb, hgrn_norm_g, lru_conv_w, lru_conv_b, lru_wr,
              lru_br, lru_wi, lru_bi, lru_lambda, ssd_conv_w, ssd_conv_b, ssd_dt_bias, ssd_a_log, ssd_d, ssd_norm_g,
              ffn_w1, ffn_w3, ffn_w2, final_norm_g):
    p = dict(w_mod=w_mod, b_mod=b_mod, norm1_g=norm1_g, norm2_g=norm2_g, w_in=w_in, w_out=w_out,
             gmlp_norm_g=gmlp_norm_g, gmlp_ws=gmlp_ws, gmlp_bs=gmlp_bs, hgrn_lb=hgrn_lb, hgrn_norm_g=hgrn_norm_g,
             lru_conv_w=lru_conv_w, lru_conv_b=lru_conv_b, lru_wr=lru_wr, lru_br=lru_br, lru_wi=lru_wi,
             lru_bi=lru_bi, lru_lambda=lru_lambda, ssd_conv_w=ssd_conv_w, ssd_conv_b=ssd_conv_b,
             ssd_dt_bias=ssd_dt_bias, ssd_a_log=ssd_a_log, ssd_d=ssd_d, ssd_norm_g=ssd_norm_g,
             ffn_w1=ffn_w1, ffn_w3=ffn_w3, ffn_w2=ffn_w2)
    f32 = jnp.float32

    bp = x_prompt.shape[0]
    z_hg = jnp.zeros((bp, 2, HG_H, HG_DK, HG_DV), f32)
    z_lru = jnp.zeros((bp, 2, G_W), f32)
    z_ssd = jnp.zeros((bp, 2, SSD_H, SSD_P, SSD_N), f32)
    x = x_prompt
    hg_list, lru_list, ssd_list = [], [], []
    for l in range(DEPTH):
        x, s_hg, s_lru, s_ssd = trunk_layer(x, c_ctx[None, :], p, l, None, z_hg, z_lru, z_ssd)
        hg_list.append(s_hg)
        lru_list.append(s_lru)
        ssd_list.append(s_ssd)
    y_prompt = rmsnorm(x, final_norm_g)
    new_state_hgrn = jnp.stack(hg_list, axis=1)
    new_state_rglru = jnp.stack(lru_list, axis=1)
    new_state_ssd = jnp.stack(ssd_list, axis=1)

    rows = x_sample.shape[1] // GRID_W
    x = x_sample
    for l in range(DEPTH):
        x, _, _, _ = trunk_layer(x, c, p, l, rows, state_hgrn[:, l], state_rglru[:, l], state_ssd[:, l])
    y_sample = rmsnorm(x, final_norm_g)

    return (y_prompt, y_sample, new_state_hgrn, new_state_rglru, new_state_ssd)
```

```python
import functools

import numpy as np
import jax
import jax.numpy as jnp
from jax import lax
from jax.experimental import pallas as pl
from jax.experimental.pallas import tpu as pltpu

F32 = jnp.float32
BF16 = jnp.bfloat16

EPS = 1e-6
N_GROUPS = 4
GRID_W = 64
A_CHUNK = 128
A_H = 4
HG_H = 4
LRU_H = 8
LRU_C = 8.0
CONV_K = 4
CONV_LEFT = 2
SSD_H = 8
SSD_G = 2
SSD_N = 128
LANE = 128
SUBLANE = 8
CHUNK = 128
VMEM_LIMIT = 56 * 1024 * 1024
NEG_BIG = -1e30


def _dot(a, b):
    return jnp.dot(a, b, preferred_element_type=F32)


def _dot_nt(a, b):
    return lax.dot_general(a, b, (((1,), (1,)), ((), ())), preferred_element_type=F32)


def _dot_tn(a, b):
    return lax.dot_general(a, b, (((0,), (0,)), ((), ())), preferred_element_type=F32)


def _dot01(m01, x):
    hi = x.astype(BF16)
    r = x - hi.astype(F32)
    mid = r.astype(BF16)
    lo = (r - mid.astype(F32)).astype(BF16)
    return _dot(m01, hi) + _dot(m01, mid) + _dot(m01, lo)


def _dot01_r(x, m01):
    hi = x.astype(BF16)
    r = x - hi.astype(F32)
    mid = r.astype(BF16)
    lo = (r - mid.astype(F32)).astype(BF16)
    return _dot(hi, m01) + _dot(mid, m01) + _dot(lo, m01)


def _sigmoid(x):
    return jax.nn.sigmoid(x)


def _silu(x):
    return x * jax.nn.sigmoid(x)


def _softplus(x):
    return jnp.maximum(x, 0.0) + jnp.log1p(jnp.exp(-jnp.abs(x)))


def _log_sigmoid(x):
    return jnp.minimum(x, 0.0) - jnp.log1p(jnp.exp(-jnp.abs(x)))


def _rms(x, g):
    return x * lax.rsqrt(jnp.mean(x * x, axis=-1, keepdims=True) + EPS) * g


def _conv_block(x, w, b, pos):
    width = x.shape[0]
    y = b + w[CONV_LEFT:CONV_LEFT + 1] * x
    for j in range(CONV_K):
        s = j - CONV_LEFT
        if s == 0:
            continue
        xs = pltpu.roll(x, (-s) % width, 0)
        ok = (pos + s >= 0) & (pos + s < width)
        y = y + w[j:j + 1] * jnp.where(ok, xs, 0.0)
    return y


def _cparams(*sem):
    return pltpu.CompilerParams(dimension_semantics=sem, vmem_limit_bytes=VMEM_LIMIT)


def _token_tile(t, seq_len, shared, cap):
    tm = min(cap, t if shared else seq_len)
    assert t % tm == 0 and (shared or seq_len % tm == 0)
    return tm


def _mod_kernel(c_ref, w_ref, b_ref, o_ref):
    s = _silu(c_ref[...]).astype(BF16)
    o_ref[...] = _dot(s, w_ref[...].astype(BF16)) + b_ref[...]


def _modulation(cond, w_mod, b_mod):
    depth, d, n = w_mod.shape
    rows = cond.shape[0]
    tn = 1024
    return pl.pallas_call(
        _mod_kernel,
        grid=(depth, n // tn),
        in_specs=[pl.BlockSpec((rows, d), lambda l, j: (0, 0)),
                  pl.BlockSpec((None, d, tn), lambda l, j: (l, 0, j)),
                  pl.BlockSpec((None, 1, tn), lambda l, j: (l, 0, j))],
        out_specs=pl.BlockSpec((None, rows, tn), lambda l, j: (l, 0, j)),
        out_shape=jax.ShapeDtypeStruct((depth, rows, n), F32),
        compiler_params=_cparams("arbitrary", "arbitrary"),
        name="modulation",
    )(cond, w_mod, b_mod.reshape(depth, 1, n))


def _in_proj_kernel(x_ref, g_ref, sh_ref, sc_ref, w_ref, o_ref, h_scr):
    @pl.when(pl.program_id(1) == 0)
    def _():
        h = _rms(x_ref[...], g_ref[...]) * (1.0 + sc_ref[...]) + sh_ref[...]
        h_scr[...] = h.astype(BF16)

    o_ref[...] = _dot(h_scr[...], w_ref[...])


def _in_proj(x2d, g, sh, sc, w_in_p, layer, seq_len):
    t, d = x2d.shape
    n = w_in_p.shape[-1]
    shared = sh.shape[0] == 1
    tm = _token_tile(t, seq_len, shared, 1024)
    tn = n // 7
    per_seq = seq_len // tm
    mod_map = (lambda i, j: (0, 0, 0)) if shared else (lambda i, j: (i // per_seq, 0, 0))
    return pl.pallas_call(
        _in_proj_kernel,
        grid=(t // tm, n // tn),
        in_specs=[pl.BlockSpec((tm, d), lambda i, j: (i, 0)),
                  pl.BlockSpec((None, 1, d), lambda i, j: (layer, 0, 0)),
                  pl.BlockSpec((None, 1, d), mod_map),
                  pl.BlockSpec((None, 1, d), mod_map),
                  pl.BlockSpec((None, d, tn), lambda i, j: (layer, 0, j))],
        out_specs=pl.BlockSpec((tm, tn), lambda i, j: (i, j)),
        out_shape=jax.ShapeDtypeStruct((t, n), F32),
        scratch_shapes=[pltpu.VMEM((tm, d), BF16)],
        compiler_params=_cparams("arbitrary", "arbitrary"),
        name="in_proj",
    )(x2d, g, sh, sc, w_in_p)


def _gmlp_kernel(u_ref, v_ref, g_ref, ws_ref, bs_ref, o_ref):
    v = _rms(jax.nn.gelu(v_ref[...]), g_ref[...]).astype(BF16)
    u = jax.nn.gelu(u_ref[...])
    tm, gw = u.shape
    hd = gw // A_H
    for c in range(tm // A_CHUNK):
        rows = slice(c * A_CHUNK, (c + 1) * A_CHUNK)
        for h in range(A_H):
            cols = slice(h * hd, (h + 1) * hd)
            r = _dot(ws_ref[h], v[rows, cols]) + bs_ref[h]
            o_ref[rows, cols] = (u[rows, cols] * r).astype(BF16)


def _gmlp(proj2d, g, ws_b, bs_full, layer, seq_len):
    t = proj2d.shape[0]
    gw = g.shape[-1]
    tm = _token_tile(t, seq_len, True, 512)
    return pl.pallas_call(
        _gmlp_kernel,
        grid=(t // tm,),
        in_specs=[pl.BlockSpec((tm, gw), lambda i: (i, 0)),
                  pl.BlockSpec((tm, gw), lambda i: (i, 1)),
                  pl.BlockSpec((None, 1, gw), lambda i: (layer, 0, 0)),
                  pl.BlockSpec((None, A_H, A_CHUNK, A_CHUNK), lambda i: (layer, 0, 0, 0)),
                  pl.BlockSpec((None, A_H, A_CHUNK, gw // A_H), lambda i: (layer, 0, 0, 0))],
        out_specs=pl.BlockSpec((tm, gw), lambda i: (i, 0)),
        out_shape=jax.ShapeDtypeStruct((t, gw), BF16),
        compiler_params=_cparams("arbitrary"),
        name="gmlp",
    )(proj2d, proj2d, g, ws_b, bs_full)


HG_LEVELS = int(np.log2(CHUNK))


def _hgrn_tables():
    c = CHUNK
    dm = np.zeros((2, (2 + HG_LEVELS) * c, c), np.float32)
    mk = np.zeros((2, HG_LEVELS, c, c), np.float32)
    u = np.arange(c)
    for r in range(c):
        dm[0, r] = u <= r
        dm[0, c + r] = u > r
        dm[1, r] = u >= r
        dm[1, c + r] = u < r
    for lv in range(HG_LEVELS):
        m = c >> (lv + 1)
        for r in range(c):
            p = r - r % (2 * m) + m
            base = (2 + lv) * c + r
            if r >= p:
                dm[0, base] = (u >= p) & (u <= r)
                dm[1, base] = (u >= p) & (u < r)
            else:
                dm[0, base] = (u > r) & (u < p)
                dm[1, base] = (u >= r) & (u < p)
        blk = u // (2 * m)
        upper = (u % (2 * m)) >= m
        same = blk[:, None] == blk[None, :]
        mk[0, lv] = same & upper[:, None] & ~upper[None, :]
        mk[1, lv] = same & ~upper[:, None] & upper[None, :]
    return dm, mk


def _hgrn_kernel(q_ref, ff_ref, fb_ref, i_ref, gt_ref, lb_ref, ng_ref, s0_ref, d_ref, m_ref,
                 y_ref, sn_ref, of_scr, st_scr, *, seq_len):
    c = CHUNK
    n_chunks = seq_len // c
    lb = lb_ref[...]

    def chunk_step(d, r0):
        rows = pl.ds(r0, c)
        q = _silu(q_ref[rows, :])
        v = i_ref[rows, :]
        f = lb + (1.0 - lb) * _sigmoid((ff_ref if d == 0 else fb_ref)[rows, :])
        kk = 1.0 - f
        e = jnp.exp(_dot01(d_ref[d], jnp.log(f)))
        qd = q * e[0:c]
        kd = kk * e[c:2 * c]
        dec = e[c - 1:c] if d == 0 else e[0:1]
        scores = jnp.zeros((c, c), F32)
        for lv in range(HG_LEVELS):
            el = e[(2 + lv) * c:(3 + lv) * c]
            scores = scores + _dot_nt((q * el).astype(BF16), (kk * el).astype(BF16)) * m_ref[d, lv]
        vb = v.astype(BF16)
        st = st_scr[...]
        o = (_dot(scores.astype(BF16), vb) + _dot_nt(qd.astype(BF16), st.astype(BF16))
             + jnp.sum(q * kk, axis=-1, keepdims=True) * v)
        st_scr[...] = st * dec + _dot_tn(vb, kd.astype(BF16))
        return o

    st_scr[...] = s0_ref[0].T

    def fwd_body(i, carry):
        r0 = pl.multiple_of(i * c, c)
        of_scr[pl.ds(r0, c), :] = chunk_step(0, r0)
        return carry

    lax.fori_loop(0, n_chunks, fwd_body, 0)
    sn_ref[0] = st_scr[...].T
    st_scr[...] = s0_ref[1].T

    def bwd_body(i, carry):
        r0 = pl.multiple_of((n_chunks - 1 - i) * c, c)
        rows = pl.ds(r0, c)
        o = of_scr[rows, :] + chunk_step(1, r0)
        y_ref[rows, :] = (_rms(o, ng_ref[...]) * _silu(gt_ref[rows, :])).astype(BF16)
        return carry

    lax.fori_loop(0, n_chunks, bwd_body, 0)
    sn_ref[1] = st_scr[...].T


def _hgrn(proj3d, lb, ng, s0, d_tab, m_tab, layer, gw_blocks):
    b, seq_len, _ = proj3d.shape
    hd = LANE
    base = 2 * gw_blocks
    col = lambda k: (lambda bi, h: (bi, 0, base + k * gw_blocks + h))
    specs = [pl.BlockSpec((None, seq_len, hd), col(k)) for k in range(5)]
    n_rows = d_tab.shape[1]
    return pl.pallas_call(
        functools.partial(_hgrn_kernel, seq_len=seq_len),
        grid=(b, HG_H),
        in_specs=specs + [
            pl.BlockSpec((None, 1, hd), lambda bi, h: (layer, 0, h)),
            pl.BlockSpec((None, 1, hd), lambda bi, h: (layer, 0, 0)),
            pl.BlockSpec((None, 2, None, hd, hd), lambda bi, h: (bi, 0, h, 0, 0)),
            pl.BlockSpec((2, n_rows, CHUNK), lambda bi, h: (0, 0, 0)),
            pl.BlockSpec((2, HG_LEVELS, CHUNK, CHUNK), lambda bi, h: (0, 0, 0, 0))],
        out_specs=[pl.BlockSpec((None, seq_len, hd), lambda bi, h: (bi, 0, h)),
                   pl.BlockSpec((None, 2, None, hd, hd), lambda bi, h: (bi, 0, h, 0, 0))],
        out_shape=[jax.ShapeDtypeStruct((b, seq_len, HG_H * hd), BF16),
                   jax.ShapeDtypeStruct((b, 2, HG_H, hd, hd), F32)],
        scratch_shapes=[pltpu.VMEM((seq_len, hd), F32), pltpu.VMEM((hd, hd), F32)],
        compiler_params=_cparams("arbitrary", "arbitrary"),
        name="hgrn",
    )(proj3d, proj3d, proj3d, proj3d, proj3d, lb, ng, s0, d_tab, m_tab)


SCAN_ROWS = 64


def _scan_block(a_blk, b_blk, carry, row, reverse):
    n_tiles = a_blk.shape[0] // SUBLANE
    outs = [None] * n_tiles
    order = range(n_tiles - 1, -1, -1) if reverse else range(n_tiles)
    for i in order:
        a = a_blk[i * SUBLANE:(i + 1) * SUBLANE]
        bb = b_blk[i * SUBLANE:(i + 1) * SUBLANE]
        d = 1
        while d < SUBLANE:
            shift = SUBLANE - d if reverse else d
            ok = (row < SUBLANE - d) if reverse else (row >= d)
            a_s = jnp.where(ok, pltpu.roll(a, shift, 0), 1.0)
            b_s = jnp.where(ok, pltpu.roll(bb, shift, 0), 0.0)
            bb = a * b_s + bb
            a = a * a_s
            d *= 2
        h = bb + a * carry
        carry = h[0:1] if reverse else h[SUBLANE - 1:SUBLANE]
        outs[i] = h
    return jnp.concatenate(outs, axis=0), carry


def _lru_kernel(cx_ref, cg_ref, cw_ref, cb_ref, w_ref, bias_ref, lam_ref, h0_ref,
                y_ref, hn_ref, a_scr, b_scr, hf_scr, *, seq_len, width):
    lanes = cx_ref.shape[-1]
    pos = lax.broadcasted_iota(jnp.int32, (width, lanes), 0)
    lsl = LRU_C * _log_sigmoid(lam_ref[...])

    def prep(i, carry):
        rows = pl.ds(pl.multiple_of(i * width, width), width)
        xc = _conv_block(cx_ref[rows, :], cw_ref[...], cb_ref[...], pos)
        gates = _sigmoid(_dot(xc.astype(BF16), w_ref[...]) + bias_ref[...])
        for d in range(2):
            r = gates[:, (2 * d) * lanes:(2 * d + 1) * lanes]
            ig = gates[:, (2 * d + 1) * lanes:(2 * d + 2) * lanes]
            log_a = r * lsl[d:d + 1]
            a = jnp.exp(log_a)
            bx = jnp.sqrt(-jnp.tanh(log_a) * (a * a + 1.0)) * (ig * xc)
            a_scr[d, rows, :] = a
            b_scr[d, rows, :] = bx
        return carry

    lax.fori_loop(0, seq_len // width, prep, 0)

    row = lax.broadcasted_iota(jnp.int32, (SUBLANE, lanes), 0)
    n_blocks = seq_len // SCAN_ROWS

    def fwd(i, carry):
        rows = pl.ds(pl.multiple_of(i * SCAN_ROWS, SCAN_ROWS), SCAN_ROWS)
        h, carry = _scan_block(a_scr[0, rows, :], b_scr[0, rows, :], carry, row, False)
        hf_scr[rows, :] = h
        return carry

    hn_ref[0:1, :] = lax.fori_loop(0, n_blocks, fwd, h0_ref[0:1, :])

    def bwd(i, carry):
        rows = pl.ds(pl.multiple_of((n_blocks - 1 - i) * SCAN_ROWS, SCAN_ROWS), SCAN_ROWS)
        h, carry = _scan_block(a_scr[1, rows, :], b_scr[1, rows, :], carry, row, True)
        y_ref[rows, :] = ((hf_scr[rows, :] + h) * jax.nn.gelu(cg_ref[rows, :])).astype(BF16)
        return carry

    hn_ref[1:2, :] = lax.fori_loop(0, n_blocks, bwd, h0_ref[1:2, :])


def _lru(proj3d, cw, cb, wcat, bcat, lam, h0, layer, width, gw_blocks):
    b, seq_len, _ = proj3d.shape
    n_grp = gw_blocks
    base = 7 * gw_blocks
    return pl.pallas_call(
        functools.partial(_lru_kernel, seq_len=seq_len, width=width),
        grid=(b, n_grp),
        in_specs=[pl.BlockSpec((None, seq_len, LANE), lambda bi, j: (bi, 0, base + j)),
                  pl.BlockSpec((None, seq_len, LANE), lambda bi, j: (bi, 0, base + gw_blocks + j)),
                  pl.BlockSpec((None, CONV_K, LANE), lambda bi, j: (layer, 0, j)),
                  pl.BlockSpec((None, 1, LANE), lambda bi, j: (layer, 0, j)),
                  pl.BlockSpec((None, None, LANE, 4 * LANE), lambda bi, j: (layer, j, 0, 0)),
                  pl.BlockSpec((None, None, 1, 4 * LANE), lambda bi, j: (layer, j, 0, 0)),
                  pl.BlockSpec((None, 2, LANE), lambda bi, j: (layer, 0, j)),
                  pl.BlockSpec((None, 2, LANE), lambda bi, j: (bi, 0, j))],
        out_specs=[pl.BlockSpec((None, seq_len, LANE), lambda bi, j: (bi, 0, j)),
                   pl.BlockSpec((None, 2, LANE), lambda bi, j: (bi, 0, j))],
        out_shape=[jax.ShapeDtypeStruct((b, seq_len, n_grp * LANE), BF16),
                   jax.ShapeDtypeStruct((b, 2, n_grp * LANE), F32)],
        scratch_shapes=[pltpu.VMEM((2, seq_len, LANE), F32), pltpu.VMEM((2, seq_len, LANE), F32),
                        pltpu.VMEM((seq_len, LANE), F32)],
        compiler_params=_cparams("arbitrary", "arbitrary"),
        name="rglru",
    )(proj3d, proj3d, cw, cb, wcat, bcat, lam, h0)


def _ssd_kernel(z_ref, x_ref, bm_ref, cm_ref, dt_ref, dtt_ref,
                cwx_ref, cwb_ref, cwc_ref, cbx_ref, cbb_ref, cbc_ref,
                ex_ref, dtb_ref, alog_ref, dtbc_ref, alogc_ref, dd_ref, ng_ref, tri_ref, h0_ref,
                y_ref, hn_ref, xs_scr, bs_scr, cs_scr, ya_scr, st_scr, *, seq_len, width):
    c = CHUNK
    n_chunks = seq_len // c
    gp = x_ref.shape[-1]
    n_hg = SSD_H // SSD_G
    hp = gp // n_hg
    pos_x = lax.broadcasted_iota(jnp.int32, (width, gp), 0)
    pos_n = lax.broadcasted_iota(jnp.int32, (width, SSD_N), 0)

    def prep(i, carry):
        rows = pl.ds(pl.multiple_of(i * width, width), width)
        xs_scr[rows, :] = _silu(_conv_block(x_ref[rows, :], cwx_ref[...], cbx_ref[...], pos_x))
        bs_scr[rows, :] = _silu(_conv_block(bm_ref[rows, :], cwb_ref[...], cbb_ref[...], pos_n))
        cs_scr[rows, :] = _silu(_conv_block(cm_ref[rows, :], cwc_ref[...], cbc_ref[...], pos_n))
        return carry

    lax.fori_loop(0, seq_len // width, prep, 0)

    ri = lax.broadcasted_iota(jnp.int32, (c, c), 0)
    ci = lax.broadcasted_iota(jnp.int32, (c, c), 1)
    lane_head = lax.broadcasted_iota(jnp.int32, (1, gp), 1) // hp

    def chunk_step(d, r0):
        rows = pl.ds(r0, c)
        tri_col = tri_ref[d]
        tri_row = tri_ref[1 - d]
        dt_e = _softplus(_dot01_r(dt_ref[rows, :], ex_ref[d]) + dtb_ref[d:d + 1])
        a_e = -jnp.exp(alog_ref[d:d + 1]) * dt_e
        acum = _dot01(tri_col, a_e)
        tot = acum[c - 1:c] if d == 0 else acum[0:1]
        acum_x = acum[:, :gp]
        dt_r = _softplus(dtt_ref[r0 // c] + dtbc_ref[...])
        acum_r = _dot01_r(-jnp.exp(alogc_ref[...]) * dt_r, tri_row)
        xs = xs_scr[rows, :]
        bm = bs_scr[rows, :].astype(BF16)
        cm = cs_scr[rows, :].astype(BF16)
        xd = xs * dt_e[:, :gp]
        xdb = xd.astype(BF16)
        g = _dot_nt(cm, bm)
        causal = (ri >= ci) if d == 0 else (ri <= ci)
        y = jnp.zeros((c, gp), F32)
        for h in range(n_hg):
            col = acum[:, gp + h * LANE:gp + (h + 1) * LANE]
            rw = acum_r[d * n_hg + h:d * n_hg + h + 1, :]
            decay = jnp.exp(jnp.where(causal, col - rw, NEG_BIG))
            xh = jnp.where(lane_head == h, xdb, jnp.zeros_like(xdb))
            y = y + _dot((g * decay).astype(BF16), xh)
        st = st_scr[...]
        y = y + _dot(cm, st.astype(BF16)) * jnp.exp(acum_x)
        to_end = jnp.exp(tot[:, :gp] - acum_x)
        st_scr[...] = st * jnp.exp(tot[:, :gp]) + _dot_tn(bm, (xd * to_end).astype(BF16))
        return y

    st_scr[...] = h0_ref[0].reshape(gp, SSD_N).T

    def fwd_body(i, carry):
        r0 = pl.multiple_of(i * c, c)
        ya_scr[pl.ds(r0, c), :] = chunk_step(0, r0)
        return carry

    lax.fori_loop(0, n_chunks, fwd_body, 0)
    hn_ref[0] = st_scr[...].T.reshape(n_hg, hp, SSD_N)
    st_scr[...] = h0_ref[1].reshape(gp, SSD_N).T

    def bwd_body(i, carry):
        r0 = pl.multiple_of((n_chunks - 1 - i) * c, c)
        rows = pl.ds(r0, c)
        y = ya_scr[rows, :] + chunk_step(1, r0) + dd_ref[...] * xs_scr[rows, :]
        y = y * _silu(z_ref[rows, :])
        y_ref[rows, :] = _rms(y, ng_ref[...]).astype(BF16)
        return carry

    lax.fori_loop(0, n_chunks, bwd_body, 0)
    hn_ref[1] = st_scr[...].T.reshape(n_hg, hp, SSD_N)


def _ssd(proj3d, dtt, cw, cb, ex, dtb, alog, dtbc, alogc, dd, ng, tri, h0, layer, width, gw_blocks):
    b, seq_len, _ = proj3d.shape
    gw = gw_blocks * LANE
    gp = gw // SSD_G
    n_hg = SSD_H // SSD_G
    hp = gp // n_hg
    xw = gp // LANE
    base_z = 9 * gw_blocks
    base_x = 10 * gw_blocks
    base_b = 11 * gw_blocks
    nb = SSD_N // LANE
    base_dt = base_b + 2 * SSD_G * nb
    e_w = ex.shape[-1]
    return pl.pallas_call(
        functools.partial(_ssd_kernel, seq_len=seq_len, width=width),
        grid=(b, SSD_G),
        in_specs=[pl.BlockSpec((None, seq_len, gp), lambda bi, g: (bi, 0, base_z // xw + g)),
                  pl.BlockSpec((None, seq_len, gp), lambda bi, g: (bi, 0, base_x // xw + g)),
                  pl.BlockSpec((None, seq_len, SSD_N), lambda bi, g: (bi, 0, base_b + g)),
                  pl.BlockSpec((None, seq_len, SSD_N), lambda bi, g: (bi, 0, base_b + SSD_G * nb + g)),
                  pl.BlockSpec((None, seq_len, LANE), lambda bi, g: (bi, 0, base_dt)),
                  pl.BlockSpec((None, None, seq_len // CHUNK, 2 * n_hg, CHUNK), lambda bi, g: (bi, g, 0, 0, 0)),
                  pl.BlockSpec((None, CONV_K, gp), lambda bi, g: (layer, 0, g)),
                  pl.BlockSpec((None, CONV_K, SSD_N), lambda bi, g: (layer, 0, gw // SSD_N + g)),
                  pl.BlockSpec((None, CONV_K, SSD_N), lambda bi, g: (layer, 0, gw // SSD_N + SSD_G + g)),
                  pl.BlockSpec((None, 1, gp), lambda bi, g: (layer, 0, g)),
                  pl.BlockSpec((None, 1, SSD_N), lambda bi, g: (layer, 0, gw // SSD_N + g)),
                  pl.BlockSpec((None, 1, SSD_N), lambda bi, g: (layer, 0, gw // SSD_N + SSD_G + g)),
                  pl.BlockSpec((None, 2, LANE, e_w), lambda bi, g: (g, 0, 0, 0)),
                  pl.BlockSpec((None, None, 2, e_w), lambda bi, g: (layer, g, 0, 0)),
                  pl.BlockSpec((None, None, 2, e_w), lambda bi, g: (layer, g, 0, 0)),
                  pl.BlockSpec((None, None, 2 * n_hg, 1), lambda bi, g: (layer, g, 0, 0)),
                  pl.BlockSpec((None, None, 2 * n_hg, 1), lambda bi, g: (layer, g, 0, 0)),
                  pl.BlockSpec((None, 1, gp), lambda bi, g: (layer, 0, g)),
                  pl.BlockSpec((None, 1, gp), lambda bi, g: (layer, 0, g)),
                  pl.BlockSpec((2, CHUNK, CHUNK), lambda bi, g: (0, 0, 0)),
                  pl.BlockSpec((None, 2, n_hg, hp, SSD_N), lambda bi, g: (bi, 0, g, 0, 0))],
        out_specs=[pl.BlockSpec((None, seq_len, gp), lambda bi, g: (bi, 0, g)),
                   pl.BlockSpec((None, 2, n_hg, hp, SSD_N), lambda bi, g: (bi, 0, g, 0, 0))],
        out_shape=[jax.ShapeDtypeStruct((b, seq_len, gw), BF16),
                   jax.ShapeDtypeStruct((b, 2, SSD_H, hp, SSD_N), F32)],
        scratch_shapes=[pltpu.VMEM((seq_len, gp), F32), pltpu.VMEM((seq_len, SSD_N), F32),
                        pltpu.VMEM((seq_len, SSD_N), F32), pltpu.VMEM((seq_len, gp), F32),
                        pltpu.VMEM((SSD_N, gp), F32)],
        compiler_params=_cparams("arbitrary", "arbitrary"),
        name="ssd",
    )(proj3d, proj3d, proj3d, proj3d, proj3d, dtt, cw, cw, cw, cb, cb, cb,
      ex, dtb, alog, dtbc, alogc, dd, ng, tri, h0)


def _out_proj_kernel(x_ref, ya_ref, yb_ref, yc_ref, yd_ref, gate_ref, w_ref, o_ref):
    gw = ya_ref.shape[-1]
    acc = _dot(ya_ref[...], w_ref[0:gw, :])
    for k, r in enumerate((yb_ref, yc_ref, yd_ref), start=1):
        acc = acc + _dot(r[...], w_ref[k * gw:(k + 1) * gw, :])
    o_ref[...] = x_ref[...] + gate_ref[...] * acc


def _out_proj(x2d, ys, gate, w_out_b, layer, seq_len):
    t, d = x2d.shape
    gw = ys[0].shape[-1]
    shared = gate.shape[0] == 1
    tm = _token_tile(t, seq_len, shared, 512)
    per_seq = seq_len // tm
    mod_map = (lambda i: (0, 0, 0)) if shared else (lambda i: (i // per_seq, 0, 0))
    y_spec = pl.BlockSpec((tm, gw), lambda i: (i, 0))
    return pl.pallas_call(
        _out_proj_kernel,
        grid=(t // tm,),
        in_specs=[pl.BlockSpec((tm, d), lambda i: (i, 0)), y_spec, y_spec, y_spec, y_spec,
                  pl.BlockSpec((None, 1, d), mod_map),
                  pl.BlockSpec((None, N_GROUPS * gw, d), lambda i: (layer, 0, 0))],
        out_specs=pl.BlockSpec((tm, d), lambda i: (i, 0)),
        out_shape=jax.ShapeDtypeStruct((t, d), F32),
        compiler_params=_cparams("arbitrary"),
        name="out_proj",
    )(x2d, *ys, gate, w_out_b)


def _ffn_kernel(x_ref, g_ref, sh_ref, sc_ref, gate_ref, w1_ref, w3_ref, w2_ref, fg_ref, o_ref,
                h_scr, acc_scr, *, final_norm):
    j = pl.program_id(1)

    @pl.when(j == 0)
    def _():
        h = _rms(x_ref[...], g_ref[...]) * (1.0 + sc_ref[...]) + sh_ref[...]
        h_scr[...] = h.astype(BF16)
        acc_scr[...] = jnp.zeros_like(acc_scr)

    h = h_scr[...]
    act = (_silu(_dot(h, w1_ref[...])) * _dot(h, w3_ref[...])).astype(BF16)
    acc_scr[...] += _dot(act, w2_ref[...])

    @pl.when(j == pl.num_programs(1) - 1)
    def _():
        xo = x_ref[...] + gate_ref[...] * acc_scr[...]
        if final_norm:
            xo = _rms(xo, fg_ref[...])
        o_ref[...] = xo


def _ffn(x2d, g, sh, sc, gate, w1_b, w3_b, w2_b, fg, layer, seq_len, final_norm):
    t, d = x2d.shape
    dff = w1_b.shape[-1]
    shared = gate.shape[0] == 1
    tm = _token_tile(t, seq_len, shared, 512)
    tf = 512
    per_seq = seq_len // tm
    mod_map = (lambda i, j: (0, 0, 0)) if shared else (lambda i, j: (i // per_seq, 0, 0))
    return pl.pallas_call(
        functools.partial(_ffn_kernel, final_norm=final_norm),
        grid=(t // tm, dff // tf),
        in_specs=[pl.BlockSpec((tm, d), lambda i, j: (i, 0)),
                  pl.BlockSpec((None, 1, d), lambda i, j: (layer, 0, 0)),
                  pl.BlockSpec((None, 1, d), mod_map),
                  pl.BlockSpec((None, 1, d), mod_map),
                  pl.BlockSpec((None, 1, d), mod_map),
                  pl.BlockSpec((None, d, tf), lambda i, j: (layer, 0, j)),
                  pl.BlockSpec((None, d, tf), lambda i, j: (layer, 0, j)),
                  pl.BlockSpec((None, tf, d), lambda i, j: (layer, j, 0)),
                  pl.BlockSpec((1, d), lambda i, j: (0, 0))],
        out_specs=pl.BlockSpec((tm, d), lambda i, j: (i, 0)),
        out_shape=jax.ShapeDtypeStruct((t, d), F32),
        scratch_shapes=[pltpu.VMEM((tm, d), BF16), pltpu.VMEM((tm, d), F32)],
        compiler_params=_cparams("arbitrary", "arbitrary"),
        name="ffn",
    )(x2d, g, sh, sc, gate, w1_b, w3_b, w2_b, fg)


def _block_diag_pairs(w):
    depth, two, h, hd, _ = w.shape
    w = w.reshape(depth, two, h // 2, 2, hd, hd)
    z = jnp.zeros_like(w[:, :, :, 0])
    top = jnp.concatenate([w[:, :, :, 0], z], axis=-1)
    bot = jnp.concatenate([z, w[:, :, :, 1]], axis=-1)
    return jnp.concatenate([top, bot], axis=-2)


def _ssd_expand_table(gp):
    n_hg = SSD_H // SSD_G
    hp = gp // n_hg
    ex = np.zeros((SSD_G, 2, LANE, gp + n_hg * LANE), np.float32)
    for g in range(SSD_G):
        for d in range(2):
            for h in range(n_hg):
                src = d * SSD_H + g * n_hg + h
                ex[g, d, src, h * hp:(h + 1) * hp] = 1.0
                ex[g, d, src, gp + h * LANE:gp + (h + 1) * LANE] = 1.0
    return ex


def _expand_heads(v, gp):
    depth = v.shape[0]
    n_hg = SSD_H // SSD_G
    hp = gp // n_hg
    v = v.reshape(depth, 2, SSD_G, n_hg).transpose(0, 2, 1, 3)
    narrow = jnp.repeat(v, hp, axis=-1)
    wide = jnp.repeat(v, LANE, axis=-1)
    return jnp.concatenate([narrow, wide], axis=-1)


def _head_cols(v):
    depth = v.shape[0]
    n_hg = SSD_H // SSD_G
    v = v.reshape(depth, 2, SSD_G, n_hg).transpose(0, 2, 1, 3)
    return v.reshape(depth, SSD_G, 2 * n_hg, 1)


def kernel(x_prompt, x_sample, state_hgrn, state_rglru, state_ssd, c, c_ctx, w_mod, b_mod, norm1_g, norm2_g, w_in, w_out, gmlp_norm_g, gmlp_ws, gmlp_bs, hgrn_lb, hgrn_norm_g, lru_conv_w, lru_conv_b, lru_wr, lru_br, lru_wi, lru_bi, lru_lambda, ssd_conv_w, ssd_conv_b, ssd_dt_bias, ssd_a_log, ssd_d, ssd_norm_g, ffn_w1, ffn_w3, ffn_w2, final_norm_g):
    depth, d_model, d_in = w_in.shape
    gw = d_model // N_GROUPS
    gwb = gw // LANE
    gp = gw // SSD_G
    n_hg = SSD_H // SSD_G
    assert gw // HG_H == LANE and SSD_N == LANE and gp % LANE == 0
    d_in_pad = -(-d_in // (7 * LANE)) * (7 * LANE)

    w_in_p = jnp.pad(w_in.astype(BF16), ((0, 0), (0, 0), (0, d_in_pad - d_in)))
    w_out_b = w_out.astype(BF16)
    w1_b, w3_b, w2_b = ffn_w1.astype(BF16), ffn_w3.astype(BF16), ffn_w2.astype(BF16)
    row3 = lambda a: a.reshape(depth, 1, -1)
    ws_b = gmlp_ws.astype(BF16)
    bs_full = jnp.broadcast_to(gmlp_bs[..., None], gmlp_bs.shape + (gw // A_H,))
    lbs = jax.nn.softmax(hgrn_lb.astype(F32), axis=0)
    lb = row3(jnp.cumsum(lbs, axis=0) - lbs[0])
    d_np, m_np = _hgrn_tables()
    d_tab, m_tab = jnp.asarray(d_np, BF16), jnp.asarray(m_np, F32)
    wr_bd, wi_bd = _block_diag_pairs(lru_wr), _block_diag_pairs(lru_wi)
    wcat = jnp.concatenate([wr_bd[:, 0], wi_bd[:, 0], wr_bd[:, 1], wi_bd[:, 1]], axis=-1).astype(BF16)
    grp = lambda a: a.reshape(depth, 2, gwb, LANE)
    bcat = jnp.concatenate([grp(lru_br)[:, 0], grp(lru_bi)[:, 0], grp(lru_br)[:, 1], grp(lru_bi)[:, 1]],
                           axis=-1).reshape(depth, gwb, 1, 4 * LANE)
    ex = jnp.asarray(_ssd_expand_table(gp), BF16)
    dtb_e, alog_e = _expand_heads(ssd_dt_bias, gp), _expand_heads(ssd_a_log, gp)
    dtb_c, alog_c = _head_cols(ssd_dt_bias), _head_cols(ssd_a_log)
    dd_e = row3(jnp.repeat(ssd_d, gw // SSD_H, axis=-1))
    tri = jnp.asarray(np.stack([np.tril(np.ones((CHUNK, CHUNK), np.float32)),
                                np.triu(np.ones((CHUNK, CHUNK), np.float32))]), BF16)
    fg = final_norm_g.reshape(1, d_model)

    n_dec = c.shape[0]
    rows = -(-(n_dec + 1) // SUBLANE) * SUBLANE
    cond = jnp.zeros((rows, d_model), F32).at[:n_dec].set(c).at[n_dec].set(c_ctx)
    mod = _modulation(cond, w_mod, b_mod)

    def run_pass(x, row_lo, row_hi, width, st_hg, st_lru, st_ssd):
        b, seq_len, _ = x.shape
        x2d = x.reshape(b * seq_len, d_model)
        s_hg, s_lru, s_ssd = [], [], []
        for l in range(depth):
            m = mod[l, row_lo:row_hi].reshape(row_hi - row_lo, 1, 6, d_model)
            sh1, sc1, g1, sh2, sc2, g2 = (m[:, :, k] for k in range(6))
            proj2d = _in_proj(x2d, row3(norm1_g), sh1, sc1, w_in_p, l, seq_len)
            proj3d = proj2d.reshape(b, seq_len, d_in_pad)
            ya = _gmlp(proj2d, row3(gmlp_norm_g), ws_b, bs_full, l, seq_len)
            yb, hg = _hgrn(proj3d, lb, row3(hgrn_norm_g), st_hg[l], d_tab, m_tab, l, gwb)
            yc, lr = _lru(proj3d, lru_conv_w, row3(lru_conv_b), wcat, bcat, lru_lambda, st_lru[l], l, width, gwb)
            dt = proj3d[:, :, d_in - 2 * SSD_H:d_in]
            dtt = dt.reshape(b, seq_len // CHUNK, CHUNK, 2, SSD_G, n_hg).transpose(0, 4, 1, 3, 5, 2)
            dtt = dtt.reshape(b, SSD_G, seq_len // CHUNK, 2 * n_hg, CHUNK)
            yd, sd = _ssd(proj3d, dtt, ssd_conv_w, row3(ssd_conv_b), ex, dtb_e, alog_e, dtb_c, alog_c,
                          dd_e, row3(ssd_norm_g), tri, st_ssd[l], l, width, gwb)
            ys = [ya, yb.reshape(-1, gw), yc.reshape(-1, gw), yd.reshape(-1, gw)]
            x2d = _out_proj(x2d, ys, g1, w_out_b, l, seq_len)
            x2d = _ffn(x2d, row3(norm2_g), sh2, sc2, g2, w1_b, w3_b, w2_b, fg, l, seq_len, l == depth - 1)
            s_hg.append(hg)
            s_lru.append(lr)
            s_ssd.append(sd)
        return x2d.reshape(b, seq_len, d_model), s_hg, s_lru, s_ssd

    bp, seq, _ = x_prompt.shape
    hd = gw // HG_H
    z_hg = [jnp.zeros((bp, 2, HG_H, hd, hd), F32)] * depth
    z_lru = [jnp.zeros((bp, 2, gw), F32)] * depth
    z_ssd = [jnp.zeros((bp, 2, SSD_H, gw // SSD_H, SSD_N), F32)] * depth
    y_prompt, s_hg, s_lru, s_ssd = run_pass(x_prompt, n_dec, n_dec + 1, seq, z_hg, z_lru, z_ssd)

    st_hg = [state_hgrn[:, l] for l in range(depth)]
    st_lru = [state_rglru[:, l] for l in range(depth)]
    st_ssd = [state_ssd[:, l] for l in range(depth)]
    y_sample, _, _, _ = run_pass(x_sample, 0, n_dec, GRID_W, st_hg, st_lru, st_ssd)

    return (y_prompt, y_sample, jnp.stack(s_hg, axis=1), jnp.stack(s_lru, axis=1), jnp.stack(s_ssd, axis=1))
```

```python
import functools

import numpy as np
import jax
import jax.numpy as jnp
from jax import lax
from jax.experimental import pallas as pl
from jax.experimental.pallas import tpu as pltpu

F32 = jnp.float32
BF16 = jnp.bfloat16

EPS = 1e-6
N_GROUPS = 4
GRID_W = 64
A_CHUNK = 128
A_H = 4
HG_H = 4
LRU_H = 8
LRU_C = 8.0
CONV_K = 4
CONV_LEFT = 2
SSD_H = 8
SSD_G = 2
SSD_N = 128
LANE = 128
SUBLANE = 8
CHUNK = 128
VMEM_LIMIT = 56 * 1024 * 1024
NEG_BIG = -1e30


def _dot(a, b):
    return jnp.dot(a, b, preferred_element_type=F32)


def _dot_nt(a, b):
    return lax.dot_general(a, b, (((1,), (1,)), ((), ())), preferred_element_type=F32)


def _dot_tn(a, b):
    return lax.dot_general(a, b, (((0,), (0,)), ((), ())), preferred_element_type=F32)


def _dot01(m01, x):
    hi = x.astype(BF16)
    r = x - hi.astype(F32)
    mid = r.astype(BF16)
    lo = (r - mid.astype(F32)).astype(BF16)
    return _dot(m01, hi) + _dot(m01, mid) + _dot(m01, lo)


def _dot01_r(x, m01, terms=3):
    hi = x.astype(BF16)
    r = x - hi.astype(F32)
    mid = r.astype(BF16)
    out = _dot(hi, m01) + _dot(mid, m01)
    if terms == 3:
        out = out + _dot((r - mid.astype(F32)).astype(BF16), m01)
    return out


def _sigmoid(x):
    return 0.5 * jnp.tanh(0.5 * x) + 0.5


def _sigmoid_rel(x):
    return jnp.exp(jnp.minimum(x, 0.0) - jnp.log(1.0 + jnp.exp(-jnp.abs(x))))


def _silu(x):
    return x * _sigmoid(x)


def _softplus(x):
    return jnp.maximum(x, 0.0) + jnp.log1p(jnp.exp(-jnp.abs(x)))


def _log_sigmoid(x):
    return jnp.minimum(x, 0.0) - jnp.log1p(jnp.exp(-jnp.abs(x)))


def _rms(x, g):
    return x * lax.rsqrt(jnp.mean(x * x, axis=-1, keepdims=True) + EPS) * g


def _conv_block(x, w, b, pos, width):
    n = x.shape[0]
    y = b + w[CONV_LEFT:CONV_LEFT + 1] * x
    for j in range(CONV_K):
        s = j - CONV_LEFT
        if s == 0:
            continue
        xs = pltpu.roll(x, (-s) % n, 0)
        ok = (pos + s >= 0) & (pos + s < width)
        y = y + w[j:j + 1] * jnp.where(ok, xs, 0.0)
    return y


def _cparams(*sem):
    return pltpu.CompilerParams(dimension_semantics=sem, vmem_limit_bytes=VMEM_LIMIT)


def _token_tile(t, seq_len, shared, cap):
    tm = min(cap, t if shared else seq_len)
    assert t % tm == 0 and (shared or seq_len % tm == 0)
    return tm


def _mod_kernel(c_ref, w_ref, b_ref, o_ref):
    s = _silu(c_ref[...]).astype(BF16)
    o_ref[...] = _dot(s, w_ref[...].astype(BF16)) + b_ref[...]


def _modulation(cond, w_mod, b_mod):
    depth, d, n = w_mod.shape
    rows = cond.shape[0]
    tn = 1024
    return pl.pallas_call(
        _mod_kernel,
        grid=(depth, n // tn),
        in_specs=[pl.BlockSpec((rows, d), lambda l, j: (0, 0)),
                  pl.BlockSpec((None, d, tn), lambda l, j: (l, 0, j)),
                  pl.BlockSpec((None, 1, tn), lambda l, j: (l, 0, j))],
        out_specs=pl.BlockSpec((None, rows, tn), lambda l, j: (l, 0, j)),
        out_shape=jax.ShapeDtypeStruct((depth, rows, n), F32),
        compiler_params=_cparams("arbitrary", "arbitrary"),
        name="modulation",
    )(cond, w_mod, b_mod.reshape(depth, 1, n))


def _in_proj_kernel(x_ref, g_ref, sh_ref, sc_ref, w_ref, o_ref, h_scr):
    @pl.when(pl.program_id(1) == 0)
    def _():
        h = _rms(x_ref[...], g_ref[...]) * (1.0 + sc_ref[...]) + sh_ref[...]
        h_scr[...] = h.astype(BF16)

    o_ref[...] = _dot(h_scr[...], w_ref[...])


def _in_proj(x2d, g, sh, sc, w_in_p, layer, seq_len):
    t, d = x2d.shape
    n = w_in_p.shape[-1]
    shared = sh.shape[0] == 1
    tm = _token_tile(t, seq_len, shared, 1024)
    tn = n // 7
    per_seq = seq_len // tm
    mod_map = (lambda i, j: (0, 0, 0)) if shared else (lambda i, j: (i // per_seq, 0, 0))
    return pl.pallas_call(
        _in_proj_kernel,
        grid=(t // tm, n // tn),
        in_specs=[pl.BlockSpec((tm, d), lambda i, j: (i, 0)),
                  pl.BlockSpec((None, 1, d), lambda i, j: (layer, 0, 0)),
                  pl.BlockSpec((None, 1, d), mod_map),
                  pl.BlockSpec((None, 1, d), mod_map),
                  pl.BlockSpec((None, d, tn), lambda i, j: (layer, 0, j))],
        out_specs=pl.BlockSpec((tm, tn), lambda i, j: (i, j)),
        out_shape=jax.ShapeDtypeStruct((t, n), F32),
        scratch_shapes=[pltpu.VMEM((tm, d), BF16)],
        compiler_params=_cparams("arbitrary", "arbitrary"),
        name="in_proj",
    )(x2d, g, sh, sc, w_in_p)


def _gmlp_kernel(u_ref, v_ref, g_ref, ws_ref, bs_ref, o_ref):
    v = _rms(jax.nn.gelu(v_ref[...]), g_ref[...]).astype(BF16)
    u = jax.nn.gelu(u_ref[...])
    tm, gw = u.shape
    hd = gw // A_H
    for c in range(tm // A_CHUNK):
        rows = slice(c * A_CHUNK, (c + 1) * A_CHUNK)
        for h in range(A_H):
            cols = slice(h * hd, (h + 1) * hd)
            r = _dot(ws_ref[h], v[rows, cols]) + bs_ref[h]
            o_ref[rows, cols] = (u[rows, cols] * r).astype(BF16)


def _gmlp(proj2d, g, ws_b, bs_full, layer, seq_len):
    t = proj2d.shape[0]
    gw = g.shape[-1]
    tm = _token_tile(t, seq_len, True, 512)
    return pl.pallas_call(
        _gmlp_kernel,
        grid=(t // tm,),
        in_specs=[pl.BlockSpec((tm, gw), lambda i: (i, 0)),
                  pl.BlockSpec((tm, gw), lambda i: (i, 1)),
                  pl.BlockSpec((None, 1, gw), lambda i: (layer, 0, 0)),
                  pl.BlockSpec((None, A_H, A_CHUNK, A_CHUNK), lambda i: (layer, 0, 0, 0)),
                  pl.BlockSpec((None, A_H, A_CHUNK, gw // A_H), lambda i: (layer, 0, 0, 0))],
        out_specs=pl.BlockSpec((tm, gw), lambda i: (i, 0)),
        out_shape=jax.ShapeDtypeStruct((t, gw), BF16),
        compiler_params=_cparams("arbitrary"),
        name="gmlp",
    )(proj2d, proj2d, g, ws_b, bs_full)


HG_LEVELS = int(np.log2(CHUNK))


def _hgrn_masks():
    c = CHUNK
    mk = np.zeros((2, HG_LEVELS, c, c), np.float32)
    u = np.arange(c)
    for lv in range(HG_LEVELS):
        m = c >> (lv + 1)
        blk = u // (2 * m)
        upper = (u % (2 * m)) >= m
        same = blk[:, None] == blk[None, :]
        mk[0, lv] = same & upper[:, None] & ~upper[None, :]
        mk[1, lv] = same & ~upper[:, None] & upper[None, :]
    return mk


def _level_ref(cum, m, d, rowc, row8):
    c, hd = cum.shape
    off = m - 1 if d == 0 else m
    if m == 1:
        odd = (rowc % 2) == 1
        if d == 0:
            return jnp.where(odd, pltpu.roll(cum, 1, 0), cum)
        return jnp.where(odd, cum, pltpu.roll(cum, c - 1, 0))
    bcast = lambda r, n: jnp.broadcast_to(cum[r:r + 1, :], (n, hd))
    if 2 * m < SUBLANE:
        per = SUBLANE // (2 * m)
        pieces = []
        for v in range(c // SUBLANE):
            piece = bcast(v * SUBLANE + off, SUBLANE)
            for k in range(1, per):
                piece = jnp.where(row8 >= k * 2 * m, bcast(v * SUBLANE + k * 2 * m + off, SUBLANE), piece)
            pieces.append(piece)
        return jnp.concatenate(pieces, axis=0)
    return jnp.concatenate([bcast(k * 2 * m + off, 2 * m) for k in range(c // (2 * m))], axis=0)


def _hgrn_kernel(q_ref, ff_ref, fb_ref, i_ref, gt_ref, lb_ref, ng_ref, s0_ref, tri_ref, m_ref,
                 y_ref, sn_ref, o_scr, st_scr, *, seq_len):
    c = CHUNK
    n_chunks = seq_len // c
    hd = q_ref.shape[-1]
    lb = lb_ref[...]
    rowc = lax.broadcasted_iota(jnp.int32, (c, hd), 0)
    row8 = lax.broadcasted_iota(jnp.int32, (SUBLANE, hd), 0)
    f_refs = (ff_ref, fb_ref)

    def chunk_step(d, r0):
        rows = pl.ds(r0, c)
        q = _silu(q_ref[rows, :])
        v = i_ref[rows, :]
        f = lb + (1.0 - lb) * _sigmoid_rel(f_refs[d][rows, :])
        kk = 1.0 - f
        cum = _dot01(tri_ref[d], jnp.log(f))
        tot = cum[c - 1:c] if d == 0 else cum[0:1]
        qd = q * jnp.exp(cum)
        kd = kk * jnp.exp(tot - cum)
        scores = jnp.zeros((c, c), F32)
        for lv in range(HG_LEVELS):
            el = jnp.exp(-jnp.abs(cum - _level_ref(cum, c >> (lv + 1), d, rowc, row8)))
            scores = scores + _dot_nt((q * el).astype(BF16), (kk * el).astype(BF16)) * m_ref[d, lv]
        vb = v.astype(BF16)
        st = st_scr[d]
        o = (_dot(scores.astype(BF16), vb) + _dot_nt(qd.astype(BF16), st.astype(BF16))
             + jnp.sum(q * kk, axis=-1, keepdims=True) * v)
        st_scr[d] = st * jnp.exp(tot) + _dot_tn(vb, kd.astype(BF16))
        return o

    st_scr[0] = s0_ref[0].T
    st_scr[1] = s0_ref[1].T

    def body(i, carry):
        rf = pl.multiple_of(i * c, c)
        rb = pl.multiple_of((n_chunks - 1 - i) * c, c)
        o_scr[0, pl.ds(rf, c), :] = chunk_step(0, rf)
        o_scr[1, pl.ds(rb, c), :] = chunk_step(1, rb)
        return carry

    lax.fori_loop(0, n_chunks, body, 0, unroll=2)
    sn_ref[0] = st_scr[0].T
    sn_ref[1] = st_scr[1].T

    def finish(i, carry):
        rows = pl.ds(pl.multiple_of(i * c, c), c)
        o = o_scr[0, rows, :] + o_scr[1, rows, :]
        y_ref[rows, :] = (_rms(o, ng_ref[...]) * _silu(gt_ref[rows, :])).astype(BF16)
        return carry

    lax.fori_loop(0, n_chunks, finish, 0)


def _hgrn(proj3d, lb, ng, s0, tri, m_tab, layer, gw_blocks):
    b, seq_len, _ = proj3d.shape
    hd = LANE
    base = 2 * gw_blocks
    col = lambda k: (lambda bi, h: (bi, 0, base + k * gw_blocks + h))
    specs = [pl.BlockSpec((None, seq_len, hd), col(k)) for k in range(5)]
    return pl.pallas_call(
        functools.partial(_hgrn_kernel, seq_len=seq_len),
        grid=(b, HG_H),
        in_specs=specs + [
            pl.BlockSpec((None, 1, hd), lambda bi, h: (layer, 0, h)),
            pl.BlockSpec((None, 1, hd), lambda bi, h: (layer, 0, 0)),
            pl.BlockSpec((None, 2, None, hd, hd), lambda bi, h: (bi, 0, h, 0, 0)),
            pl.BlockSpec((2, CHUNK, CHUNK), lambda bi, h: (0, 0, 0)),
            pl.BlockSpec((2, HG_LEVELS, CHUNK, CHUNK), lambda bi, h: (0, 0, 0, 0))],
        out_specs=[pl.BlockSpec((None, seq_len, hd), lambda bi, h: (bi, 0, h)),
                   pl.BlockSpec((None, 2, None, hd, hd), lambda bi, h: (bi, 0, h, 0, 0))],
        out_shape=[jax.ShapeDtypeStruct((b, seq_len, HG_H * hd), BF16),
                   jax.ShapeDtypeStruct((b, 2, HG_H, hd, hd), F32)],
        scratch_shapes=[pltpu.VMEM((2, seq_len, hd), F32), pltpu.VMEM((2, hd, hd), F32)],
        compiler_params=_cparams("arbitrary", "arbitrary"),
        name="hgrn",
    )(proj3d, proj3d, proj3d, proj3d, proj3d, lb, ng, s0, tri, m_tab)


SCAN_ROWS = 64


def _scan_block(a_blk, b_blk, carry, row, reverse):
    n_tiles = a_blk.shape[0] // SUBLANE
    outs = [None] * n_tiles
    order = range(n_tiles - 1, -1, -1) if reverse else range(n_tiles)
    for i in order:
        a = a_blk[i * SUBLANE:(i + 1) * SUBLANE]
        bb = b_blk[i * SUBLANE:(i + 1) * SUBLANE]
        d = 1
        while d < SUBLANE:
            shift = SUBLANE - d if reverse else d
            ok = (row < SUBLANE - d) if reverse else (row >= d)
            a_s = jnp.where(ok, pltpu.roll(a, shift, 0), 1.0)
            b_s = jnp.where(ok, pltpu.roll(bb, shift, 0), 0.0)
            bb = a * b_s + bb
            a = a * a_s
            d *= 2
        h = bb + a * carry
        carry = h[0:1] if reverse else h[SUBLANE - 1:SUBLANE]
        outs[i] = h
    return jnp.concatenate(outs, axis=0), carry


def _prep_rows(seq_len, width):
    rows = max(width, min(256, seq_len))
    assert rows % width == 0 and seq_len % rows == 0
    return rows


def _lru_kernel(cx_ref, cg_ref, cw_ref, cb_ref, w_ref, bias_ref, lam_ref, h0_ref,
                y_ref, hn_ref, a_scr, b_scr, h_scr, *, seq_len, width):
    lanes = cx_ref.shape[-1]
    pr = _prep_rows(seq_len, width)
    pos = lax.broadcasted_iota(jnp.int32, (pr, lanes), 0) % width
    lsl = LRU_C * _log_sigmoid(lam_ref[...])

    def prep(i, carry):
        rows = pl.ds(pl.multiple_of(i * pr, pr), pr)
        xc = _conv_block(cx_ref[rows, :], cw_ref[...], cb_ref[...], pos, width)
        gates = _sigmoid(_dot(xc.astype(BF16), w_ref[...]) + bias_ref[...])
        for d in range(2):
            r = gates[:, (2 * d) * lanes:(2 * d + 1) * lanes]
            ig = gates[:, (2 * d + 1) * lanes:(2 * d + 2) * lanes]
            log_a = r * lsl[d:d + 1]
            a = jnp.exp(log_a)
            w = -jnp.tanh(log_a) * (a * a + 1.0)
            root = jnp.where(w > 0.0, w * lax.rsqrt(w), 0.0)
            bx = root * (ig * xc)
            a_scr[d, rows, :] = a
            b_scr[d, rows, :] = bx
        return carry

    lax.fori_loop(0, seq_len // pr, prep, 0)

    row = lax.broadcasted_iota(jnp.int32, (SUBLANE, lanes), 0)
    n_blocks = seq_len // SCAN_ROWS

    def scan(i, carry):
        cf, cb = carry
        rf = pl.ds(pl.multiple_of(i * SCAN_ROWS, SCAN_ROWS), SCAN_ROWS)
        rb = pl.ds(pl.multiple_of((n_blocks - 1 - i) * SCAN_ROWS, SCAN_ROWS), SCAN_ROWS)
        hf, cf = _scan_block(a_scr[0, rf, :], b_scr[0, rf, :], cf, row, False)
        hb, cb = _scan_block(a_scr[1, rb, :], b_scr[1, rb, :], cb, row, True)
        h_scr[0, rf, :] = hf
        h_scr[1, rb, :] = hb
        return cf, cb

    cf, cb = lax.fori_loop(0, n_blocks, scan, (h0_ref[0:1, :], h0_ref[1:2, :]))
    hn_ref[0:1, :] = cf
    hn_ref[1:2, :] = cb

    def finish(i, carry):
        rows = pl.ds(pl.multiple_of(i * pr, pr), pr)
        y_ref[rows, :] = ((h_scr[0, rows, :] + h_scr[1, rows, :]) * jax.nn.gelu(cg_ref[rows, :])).astype(BF16)
        return carry

    lax.fori_loop(0, seq_len // pr, finish, 0)


def _lru(proj3d, cw, cb, wcat, bcat, lam, h0, layer, width, gw_blocks):
    b, seq_len, _ = proj3d.shape
    n_grp = gw_blocks
    base = 7 * gw_blocks
    return pl.pallas_call(
        functools.partial(_lru_kernel, seq_len=seq_len, width=width),
        grid=(b, n_grp),
        in_specs=[pl.BlockSpec((None, seq_len, LANE), lambda bi, j: (bi, 0, base + j)),
                  pl.BlockSpec((None, seq_len, LANE), lambda bi, j: (bi, 0, base + gw_blocks + j)),
                  pl.BlockSpec((None, CONV_K, LANE), lambda bi, j: (layer, 0, j)),
                  pl.BlockSpec((None, 1, LANE), lambda bi, j: (layer, 0, j)),
                  pl.BlockSpec((None, None, LANE, 4 * LANE), lambda bi, j: (layer, j, 0, 0)),
                  pl.BlockSpec((None, None, 1, 4 * LANE), lambda bi, j: (layer, j, 0, 0)),
                  pl.BlockSpec((None, 2, LANE), lambda bi, j: (layer, 0, j)),
                  pl.BlockSpec((None, 2, LANE), lambda bi, j: (bi, 0, j))],
        out_specs=[pl.BlockSpec((None, seq_len, LANE), lambda bi, j: (bi, 0, j)),
                   pl.BlockSpec((None, 2, LANE), lambda bi, j: (bi, 0, j))],
        out_shape=[jax.ShapeDtypeStruct((b, seq_len, n_grp * LANE), BF16),
                   jax.ShapeDtypeStruct((b, 2, n_grp * LANE), F32)],
        scratch_shapes=[pltpu.VMEM((2, seq_len, LANE), F32), pltpu.VMEM((2, seq_len, LANE), F32),
                        pltpu.VMEM((2, seq_len, LANE), F32)],
        compiler_params=_cparams("arbitrary", "arbitrary"),
        name="rglru",
    )(proj3d, proj3d, cw, cb, wcat, bcat, lam, h0)


def _ssd_kernel(z_ref, x_ref, bm_ref, cm_ref, dt_ref, dtt_ref,
                cwx_ref, cwb_ref, cwc_ref, cbx_ref, cbb_ref, cbc_ref,
                ex_ref, dtb_ref, alog_ref, dtbc_ref, alogc_ref, dd_ref, ng_ref, tri_ref, h0_ref,
                y_ref, hn_ref, xs_scr, bs_scr, cs_scr, ya_scr, st_scr, *, seq_len, width):
    c = CHUNK
    n_chunks = seq_len // c
    gp = x_ref.shape[-1]
    n_hg = SSD_H // SSD_G
    hp = gp // n_hg
    pr = _prep_rows(seq_len, width)
    pos_x = lax.broadcasted_iota(jnp.int32, (pr, gp), 0) % width
    pos_n = lax.broadcasted_iota(jnp.int32, (pr, SSD_N), 0) % width

    def prep(i, carry):
        rows = pl.ds(pl.multiple_of(i * pr, pr), pr)
        xs_scr[rows, :] = _silu(_conv_block(x_ref[rows, :], cwx_ref[...], cbx_ref[...], pos_x, width))
        bs_scr[rows, :] = _silu(_conv_block(bm_ref[rows, :], cwb_ref[...], cbb_ref[...], pos_n, width))
        cs_scr[rows, :] = _silu(_conv_block(cm_ref[rows, :], cwc_ref[...], cbc_ref[...], pos_n, width))
        return carry

    lax.fori_loop(0, seq_len // pr, prep, 0)

    ri = lax.broadcasted_iota(jnp.int32, (c, c), 0)
    ci = lax.broadcasted_iota(jnp.int32, (c, c), 1)
    assert LANE == 2 * hp
    lane_s = lax.broadcasted_iota(jnp.int32, (1, LANE), 1)

    def chunk_step(d, r0):
        rows = pl.ds(r0, c)
        dt_c = _softplus(dt_ref[rows, :] + dtb_ref[...])
        cum_c = _dot01(tri_ref[d], -jnp.exp(alog_ref[...]) * dt_c)
        ex = ex_ref[d]
        dt_x = _dot01_r(dt_c, ex, terms=2)
        cum_x = _dot01_r(cum_c, ex)
        tot = cum_x[c - 1:c] if d == 0 else cum_x[0:1]
        dt_r = _softplus(dtt_ref[r0 // c] + dtbc_ref[...])
        cum_r = _dot01_r(-jnp.exp(alogc_ref[...]) * dt_r, tri_ref[1 - d])
        xs = xs_scr[rows, :]
        bm = bs_scr[rows, :].astype(BF16)
        cm = cs_scr[rows, :].astype(BF16)
        xd = xs * dt_x
        xdb = xd.astype(BF16)
        g = _dot_nt(cm, bm)
        causal = (ri >= ci) if d == 0 else (ri <= ci)
        slabs = []
        for j in range(gp // LANE):
            cum_s = cum_x[:, j * LANE:(j + 1) * LANE]
            cum_o = pltpu.roll(cum_s, hp, 1)
            xd_s = xdb[:, j * LANE:(j + 1) * LANE]
            y_s = jnp.zeros((c, LANE), F32)
            for k in range(2):
                h = 2 * j + k
                mine = (lane_s // hp) == k
                col = jnp.where(mine, cum_s, cum_o)
                rw = cum_r[d * n_hg + h:d * n_hg + h + 1, :]
                decay = jnp.exp(jnp.where(causal, col - rw, NEG_BIG))
                xh = jnp.where(mine, xd_s, jnp.zeros_like(xd_s))
                y_s = y_s + _dot((g * decay).astype(BF16), xh)
            slabs.append(y_s)
        y = jnp.concatenate(slabs, axis=1)
        st = st_scr[d]
        y = y + _dot(cm, st.astype(BF16)) * jnp.exp(cum_x)
        to_end = jnp.exp(tot - cum_x)
        st_scr[d] = st * jnp.exp(tot) + _dot_tn(bm, (xd * to_end).astype(BF16))
        return y

    st_scr[0] = h0_ref[0].reshape(gp, SSD_N).T
    st_scr[1] = h0_ref[1].reshape(gp, SSD_N).T

    def body(i, carry):
        rf = pl.multiple_of(i * c, c)
        rb = pl.multiple_of((n_chunks - 1 - i) * c, c)
        ya_scr[0, pl.ds(rf, c), :] = chunk_step(0, rf)
        ya_scr[1, pl.ds(rb, c), :] = chunk_step(1, rb)
        return carry

    lax.fori_loop(0, n_chunks, body, 0)
    hn_ref[0] = st_scr[0].T.reshape(n_hg, hp, SSD_N)
    hn_ref[1] = st_scr[1].T.reshape(n_hg, hp, SSD_N)

    def finish(i, carry):
        rows = pl.ds(pl.multiple_of(i * c, c), c)
        y = ya_scr[0, rows, :] + ya_scr[1, rows, :] + dd_ref[...] * xs_scr[rows, :]
        y = y * _silu(z_ref[rows, :])
        y_ref[rows, :] = _rms(y, ng_ref[...]).astype(BF16)
        return carry

    lax.fori_loop(0, n_chunks, finish, 0)


def _ssd(proj3d, dtt, cw, cb, ex, dtb, alog, dtbc, alogc, dd, ng, tri, h0, layer, width, gw_blocks):
    b, seq_len, _ = proj3d.shape
    gw = gw_blocks * LANE
    gp = gw // SSD_G
    n_hg = SSD_H // SSD_G
    hp = gp // n_hg
    xw = gp // LANE
    base_z = 9 * gw_blocks
    base_x = 10 * gw_blocks
    base_b = 11 * gw_blocks
    nb = SSD_N // LANE
    base_dt = base_b + 2 * SSD_G * nb
    e_w = ex.shape[-1]
    return pl.pallas_call(
        functools.partial(_ssd_kernel, seq_len=seq_len, width=width),
        grid=(b, SSD_G),
        in_specs=[pl.BlockSpec((None, seq_len, gp), lambda bi, g: (bi, 0, base_z // xw + g)),
                  pl.BlockSpec((None, seq_len, gp), lambda bi, g: (bi, 0, base_x // xw + g)),
                  pl.BlockSpec((None, seq_len, SSD_N), lambda bi, g: (bi, 0, base_b + g)),
                  pl.BlockSpec((None, seq_len, SSD_N), lambda bi, g: (bi, 0, base_b + SSD_G * nb + g)),
                  pl.BlockSpec((None, seq_len, LANE), lambda bi, g: (bi, 0, base_dt)),
                  pl.BlockSpec((None, None, seq_len // CHUNK, 2 * n_hg, CHUNK), lambda bi, g: (bi, g, 0, 0, 0)),
                  pl.BlockSpec((None, CONV_K, gp), lambda bi, g: (layer, 0, g)),
                  pl.BlockSpec((None, CONV_K, SSD_N), lambda bi, g: (layer, 0, gw // SSD_N + g)),
                  pl.BlockSpec((None, CONV_K, SSD_N), lambda bi, g: (layer, 0, gw // SSD_N + SSD_G + g)),
                  pl.BlockSpec((None, 1, gp), lambda bi, g: (layer, 0, g)),
                  pl.BlockSpec((None, 1, SSD_N), lambda bi, g: (layer, 0, gw // SSD_N + g)),
                  pl.BlockSpec((None, 1, SSD_N), lambda bi, g: (layer, 0, gw // SSD_N + SSD_G + g)),
                  pl.BlockSpec((None, 2, LANE, e_w), lambda bi, g: (g, 0, 0, 0)),
                  pl.BlockSpec((None, 1, LANE), lambda bi, g: (layer, 0, 0)),
                  pl.BlockSpec((None, 1, LANE), lambda bi, g: (layer, 0, 0)),
                  pl.BlockSpec((None, None, 2 * n_hg, 1), lambda bi, g: (layer, g, 0, 0)),
                  pl.BlockSpec((None, None, 2 * n_hg, 1), lambda bi, g: (layer, g, 0, 0)),
                  pl.BlockSpec((None, 1, gp), lambda bi, g: (layer, 0, g)),
                  pl.BlockSpec((None, 1, gp), lambda bi, g: (layer, 0, g)),
                  pl.BlockSpec((2, CHUNK, CHUNK), lambda bi, g: (0, 0, 0)),
                  pl.BlockSpec((None, 2, n_hg, hp, SSD_N), lambda bi, g: (bi, 0, g, 0, 0))],
        out_specs=[pl.BlockSpec((None, seq_len, gp), lambda bi, g: (bi, 0, g)),
                   pl.BlockSpec((None, 2, n_hg, hp, SSD_N), lambda bi, g: (bi, 0, g, 0, 0))],
        out_shape=[jax.ShapeDtypeStruct((b, seq_len, gw), BF16),
                   jax.ShapeDtypeStruct((b, 2, SSD_H, hp, SSD_N), F32)],
        scratch_shapes=[pltpu.VMEM((seq_len, gp), F32), pltpu.VMEM((seq_len, SSD_N), F32),
                        pltpu.VMEM((seq_len, SSD_N), F32), pltpu.VMEM((2, seq_len, gp), F32),
                        pltpu.VMEM((2, SSD_N, gp), F32)],
        compiler_params=_cparams("arbitrary", "arbitrary"),
        name="ssd",
    )(proj3d, proj3d, proj3d, proj3d, proj3d, dtt, cw, cw, cw, cb, cb, cb,
      ex, dtb, alog, dtbc, alogc, dd, ng, tri, h0)


def _out_proj_kernel(x_ref, ya_ref, yb_ref, yc_ref, yd_ref, gate_ref, w_ref, o_ref):
    gw = ya_ref.shape[-1]
    acc = _dot(ya_ref[...], w_ref[0:gw, :])
    for k, r in enumerate((yb_ref, yc_ref, yd_ref), start=1):
        acc = acc + _dot(r[...], w_ref[k * gw:(k + 1) * gw, :])
    o_ref[...] = x_ref[...] + gate_ref[...] * acc


def _out_proj(x2d, ys, gate, w_out_b, layer, seq_len):
    t, d = x2d.shape
    gw = ys[0].shape[-1]
    shared = gate.shape[0] == 1
    tm = _token_tile(t, seq_len, shared, 512)
    per_seq = seq_len // tm
    mod_map = (lambda i: (0, 0, 0)) if shared else (lambda i: (i // per_seq, 0, 0))
    y_spec = pl.BlockSpec((tm, gw), lambda i: (i, 0))
    return pl.pallas_call(
        _out_proj_kernel,
        grid=(t // tm,),
        in_specs=[pl.BlockSpec((tm, d), lambda i: (i, 0)), y_spec, y_spec, y_spec, y_spec,
                  pl.BlockSpec((None, 1, d), mod_map),
                  pl.BlockSpec((None, N_GROUPS * gw, d), lambda i: (layer, 0, 0))],
        out_specs=pl.BlockSpec((tm, d), lambda i: (i, 0)),
        out_shape=jax.ShapeDtypeStruct((t, d), F32),
        compiler_params=_cparams("arbitrary"),
        name="out_proj",
    )(x2d, *ys, gate, w_out_b)


def _ffn_kernel(x_ref, g_ref, sh_ref, sc_ref, gate_ref, w1_ref, w3_ref, w2_ref, fg_ref, o_ref,
                h_scr, acc_scr, *, final_norm):
    j = pl.program_id(1)

    @pl.when(j == 0)
    def _():
        h = _rms(x_ref[...], g_ref[...]) * (1.0 + sc_ref[...]) + sh_ref[...]
        h_scr[...] = h.astype(BF16)
        acc_scr[...] = jnp.zeros_like(acc_scr)

    h = h_scr[...]
    act = (_silu(_dot(h, w1_ref[...])) * _dot(h, w3_ref[...])).astype(BF16)
    acc_scr[...] += _dot(act, w2_ref[...])

    @pl.when(j == pl.num_programs(1) - 1)
    def _():
        xo = x_ref[...] + gate_ref[...] * acc_scr[...]
        if final_norm:
            xo = _rms(xo, fg_ref[...])
        o_ref[...] = xo


def _ffn(x2d, g, sh, sc, gate, w1_b, w3_b, w2_b, fg, layer, seq_len, final_norm):
    t, d = x2d.shape
    dff = w1_b.shape[-1]
    shared = gate.shape[0] == 1
    tm = _token_tile(t, seq_len, shared, 512)
    tf = 512
    per_seq = seq_len // tm
    mod_map = (lambda i, j: (0, 0, 0)) if shared else (lambda i, j: (i // per_seq, 0, 0))
    return pl.pallas_call(
        functools.partial(_ffn_kernel, final_norm=final_norm),
        grid=(t // tm, dff // tf),
        in_specs=[pl.BlockSpec((tm, d), lambda i, j: (i, 0)),
                  pl.BlockSpec((None, 1, d), lambda i, j: (layer, 0, 0)),
                  pl.BlockSpec((None, 1, d), mod_map),
                  pl.BlockSpec((None, 1, d), mod_map),
                  pl.BlockSpec((None, 1, d), mod_map),
                  pl.BlockSpec((None, d, tf), lambda i, j: (layer, 0, j)),
                  pl.BlockSpec((None, d, tf), lambda i, j: (layer, 0, j)),
                  pl.BlockSpec((None, tf, d), lambda i, j: (layer, j, 0)),
                  pl.BlockSpec((1, d), lambda i, j: (0, 0))],
        out_specs=pl.BlockSpec((tm, d), lambda i, j: (i, 0)),
        out_shape=jax.ShapeDtypeStruct((t, d), F32),
        scratch_shapes=[pltpu.VMEM((tm, d), BF16), pltpu.VMEM((tm, d), F32)],
        compiler_params=_cparams("arbitrary", "arbitrary"),
        name="ffn",
    )(x2d, g, sh, sc, gate, w1_b, w3_b, w2_b, fg)


def _block_diag_pairs(w):
    depth, two, h, hd, _ = w.shape
    w = w.reshape(depth, two, h // 2, 2, hd, hd)
    z = jnp.zeros_like(w[:, :, :, 0])
    top = jnp.concatenate([w[:, :, :, 0], z], axis=-1)
    bot = jnp.concatenate([z, w[:, :, :, 1]], axis=-1)
    return jnp.concatenate([top, bot], axis=-2)


def _ssd_expand_table(gp):
    n_hg = SSD_H // SSD_G
    hp = gp // n_hg
    ex = np.zeros((SSD_G, 2, LANE, gp), np.float32)
    for g in range(SSD_G):
        for d in range(2):
            for h in range(n_hg):
                ex[g, d, d * SSD_H + g * n_hg + h, h * hp:(h + 1) * hp] = 1.0
    return ex


def _head_lanes(v):
    depth = v.shape[0]
    flat = v.reshape(depth, 1, 2 * SSD_H)
    return jnp.pad(flat, ((0, 0), (0, 0), (0, LANE - 2 * SSD_H)))


def _head_cols(v):
    depth = v.shape[0]
    n_hg = SSD_H // SSD_G
    v = v.reshape(depth, 2, SSD_G, n_hg).transpose(0, 2, 1, 3)
    return v.reshape(depth, SSD_G, 2 * n_hg, 1)


def kernel(x_prompt, x_sample, state_hgrn, state_rglru, state_ssd, c, c_ctx, w_mod, b_mod, norm1_g, norm2_g, w_in, w_out, gmlp_norm_g, gmlp_ws, gmlp_bs, hgrn_lb, hgrn_norm_g, lru_conv_w, lru_conv_b, lru_wr, lru_br, lru_wi, lru_bi, lru_lambda, ssd_conv_w, ssd_conv_b, ssd_dt_bias, ssd_a_log, ssd_d, ssd_norm_g, ffn_w1, ffn_w3, ffn_w2, final_norm_g):
    depth, d_model, d_in = w_in.shape
    gw = d_model // N_GROUPS
    gwb = gw // LANE
    gp = gw // SSD_G
    n_hg = SSD_H // SSD_G
    assert gw // HG_H == LANE and SSD_N == LANE and gp % LANE == 0
    d_in_pad = -(-d_in // (7 * LANE)) * (7 * LANE)

    w_in_p = jnp.pad(w_in.astype(BF16), ((0, 0), (0, 0), (0, d_in_pad - d_in)))
    w_out_b = w_out.astype(BF16)
    w1_b, w3_b, w2_b = ffn_w1.astype(BF16), ffn_w3.astype(BF16), ffn_w2.astype(BF16)
    row3 = lambda a: a.reshape(depth, 1, -1)
    ws_b = gmlp_ws.astype(BF16)
    bs_full = jnp.broadcast_to(gmlp_bs[..., None], gmlp_bs.shape + (gw // A_H,))
    lbs = jax.nn.softmax(hgrn_lb.astype(F32), axis=0)
    lb = row3(jnp.cumsum(lbs, axis=0) - lbs[0])
    m_tab = jnp.asarray(_hgrn_masks(), F32)
    wr_bd, wi_bd = _block_diag_pairs(lru_wr), _block_diag_pairs(lru_wi)
    wcat = jnp.concatenate([wr_bd[:, 0], wi_bd[:, 0], wr_bd[:, 1], wi_bd[:, 1]], axis=-1).astype(BF16)
    grp = lambda a: a.reshape(depth, 2, gwb, LANE)
    bcat = jnp.concatenate([grp(lru_br)[:, 0], grp(lru_bi)[:, 0], grp(lru_br)[:, 1], grp(lru_bi)[:, 1]],
                           axis=-1).reshape(depth, gwb, 1, 4 * LANE)
    ex = jnp.asarray(_ssd_expand_table(gp), BF16)
    dtb_e, alog_e = _head_lanes(ssd_dt_bias), _head_lanes(ssd_a_log)
    dtb_c, alog_c = _head_cols(ssd_dt_bias), _head_cols(ssd_a_log)
    dd_e = row3(jnp.repeat(ssd_d, gw // SSD_H, axis=-1))
    tri = jnp.asarray(np.stack([np.tril(np.ones((CHUNK, CHUNK), np.float32)),
                                np.triu(np.ones((CHUNK, CHUNK), np.float32))]), BF16)
    fg = final_norm_g.reshape(1, d_model)

    n_dec = c.shape[0]
    rows = -(-(n_dec + 1) // SUBLANE) * SUBLANE
    cond = jnp.zeros((rows, d_model), F32).at[:n_dec].set(c).at[n_dec].set(c_ctx)
    mod = _modulation(cond, w_mod, b_mod)

    def run_pass(x, row_lo, row_hi, width, st_hg, st_lru, st_ssd):
        b, seq_len, _ = x.shape
        x2d = x.reshape(b * seq_len, d_model)
        s_hg, s_lru, s_ssd = [], [], []
        for l in range(depth):
            m = mod[l, row_lo:row_hi].reshape(row_hi - row_lo, 1, 6, d_model)
            sh1, sc1, g1, sh2, sc2, g2 = (m[:, :, k] for k in range(6))
            proj2d = _in_proj(x2d, row3(norm1_g), sh1, sc1, w_in_p, l, seq_len)
            proj3d = proj2d.reshape(b, seq_len, d_in_pad)
            ya = _gmlp(proj2d, row3(gmlp_norm_g), ws_b, bs_full, l, seq_len)
            yb, hg = _hgrn(proj3d, lb, row3(hgrn_norm_g), st_hg[l], tri, m_tab, l, gwb)
            yc, lr = _lru(proj3d, lru_conv_w, row3(lru_conv_b), wcat, bcat, lru_lambda, st_lru[l], l, width, gwb)
            dt = proj3d[:, :, d_in - 2 * SSD_H:d_in]
            dtt = dt.reshape(b, seq_len // CHUNK, CHUNK, 2, SSD_G, n_hg).transpose(0, 4, 1, 3, 5, 2)
            dtt = dtt.reshape(b, SSD_G, seq_len // CHUNK, 2 * n_hg, CHUNK)
            yd, sd = _ssd(proj3d, dtt, ssd_conv_w, row3(ssd_conv_b), ex, dtb_e, alog_e, dtb_c, alog_c,
                          dd_e, row3(ssd_norm_g), tri, st_ssd[l], l, width, gwb)
            ys = [ya, yb.reshape(-1, gw), yc.reshape(-1, gw), yd.reshape(-1, gw)]
            x2d = _out_proj(x2d, ys, g1, w_out_b, l, seq_len)
            x2d = _ffn(x2d, row3(norm2_g), sh2, sc2, g2, w1_b, w3_b, w2_b, fg, l, seq_len, l == depth - 1)
            s_hg.append(hg)
            s_lru.append(lr)
            s_ssd.append(sd)
        return x2d.reshape(b, seq_len, d_model), s_hg, s_lru, s_ssd

    bp, seq, _ = x_prompt.shape
    hd = gw // HG_H
    z_hg = [jnp.zeros((bp, 2, HG_H, hd, hd), F32)] * depth
    z_lru = [jnp.zeros((bp, 2, gw), F32)] * depth
    z_ssd = [jnp.zeros((bp, 2, SSD_H, gw // SSD_H, SSD_N), F32)] * depth
    y_prompt, s_hg, s_lru, s_ssd = run_pass(x_prompt, n_dec, n_dec + 1, seq, z_hg, z_lru, z_ssd)

    st_hg = [state_hgrn[:, l] for l in range(depth)]
    st_lru = [state_rglru[:, l] for l in range(depth)]
    st_ssd = [state_ssd[:, l] for l in range(depth)]
    y_sample, _, _, _ = run_pass(x_sample, 0, n_dec, GRID_W, st_hg, st_lru, st_ssd)

    return (y_prompt, y_sample, jnp.stack(s_hg, axis=1), jnp.stack(s_lru, axis=1), jnp.stack(s_ssd, axis=1))
```

```python
import functools

import numpy as np
import jax
import jax.numpy as jnp
from jax import lax
from jax.experimental import pallas as pl
from jax.experimental.pallas import tpu as pltpu

F32 = jnp.float32
BF16 = jnp.bfloat16

EPS = 1e-6
N_GROUPS = 4
GRID_W = 64
A_CHUNK = 128
A_H = 4
HG_H = 4
LRU_H = 8
LRU_C = 8.0
CONV_K = 4
CONV_LEFT = 2
SSD_H = 8
SSD_G = 2
SSD_N = 128
LANE = 128
SUBLANE = 8
CHUNK = 128
MXU_N = 256
IN_PROJ_TN = 5 * MXU_N
VMEM_LIMIT = 56 * 1024 * 1024
NEG_BIG = -1e30


def _dot(a, b):
    return jnp.dot(a, b, preferred_element_type=F32)


def _dot_nt(a, b):
    return lax.dot_general(a, b, (((1,), (1,)), ((), ())), preferred_element_type=F32)


def _dot_tn(a, b):
    return lax.dot_general(a, b, (((0,), (0,)), ((), ())), preferred_element_type=F32)


def _split_terms(x, terms):
    hi = x.astype(BF16).astype(F32)
    r = x - hi
    if terms == 2:
        return [hi, r]
    mid = r.astype(BF16).astype(F32)
    return [hi, mid, r - mid]


def _dot01(m01, x, terms=3):
    return sum(_dot(m01, t.astype(BF16)) for t in _split_terms(x, terms))


def _sigmoid(x):
    return 0.5 * jnp.tanh(0.5 * x) + 0.5


def _sigmoid_rel(x):
    return jnp.exp(jnp.minimum(x, 0.0) - jnp.log(1.0 + jnp.exp(-jnp.abs(x))))


def _silu(x):
    return x * _sigmoid(x)


def _softplus(x):
    return jnp.maximum(x, 0.0) + jnp.log1p(jnp.exp(-jnp.abs(x)))


def _log_sigmoid(x):
    return jnp.minimum(x, 0.0) - jnp.log1p(jnp.exp(-jnp.abs(x)))


def _rms(x, g):
    return x * lax.rsqrt(jnp.mean(x * x, axis=-1, keepdims=True) + EPS) * g


def _conv_block(x, w, b, pos, width):
    n = x.shape[0]
    y = b + w[CONV_LEFT:CONV_LEFT + 1] * x
    for j in range(CONV_K):
        s = j - CONV_LEFT
        if s == 0:
            continue
        xs = pltpu.roll(x, (-s) % n, 0)
        ok = (pos + s >= 0) & (pos + s < width)
        y = y + w[j:j + 1] * jnp.where(ok, xs, 0.0)
    return y


def _cparams(*sem):
    return pltpu.CompilerParams(dimension_semantics=sem, vmem_limit_bytes=VMEM_LIMIT)


def _token_tile(t, seq_len, shared, cap):
    tm = min(cap, t if shared else seq_len)
    assert t % tm == 0 and (shared or seq_len % tm == 0)
    return tm


def _mod_kernel(c_ref, w_ref, b_ref, o_ref):
    s = _silu(c_ref[...]).astype(BF16)
    o_ref[...] = _dot(s, w_ref[...].astype(BF16)) + b_ref[...]


def _modulation(cond, w_mod, b_mod):
    depth, d, n = w_mod.shape
    rows = cond.shape[0]
    tn = 1024
    return pl.pallas_call(
        _mod_kernel,
        grid=(depth, n // tn),
        in_specs=[pl.BlockSpec((rows, d), lambda l, j: (0, 0)),
                  pl.BlockSpec((None, d, tn), lambda l, j: (l, 0, j)),
                  pl.BlockSpec((None, 1, tn), lambda l, j: (l, 0, j))],
        out_specs=pl.BlockSpec((None, rows, tn), lambda l, j: (l, 0, j)),
        out_shape=jax.ShapeDtypeStruct((depth, rows, n), F32),
        compiler_params=_cparams("arbitrary", "arbitrary"),
        name="modulation",
    )(cond, w_mod, b_mod.reshape(depth, 1, n))


def _in_proj_kernel(x_ref, g_ref, sh_ref, sc_ref, w_ref, o_ref, h_scr):
    @pl.when(pl.program_id(1) == 0)
    def _():
        h = _rms(x_ref[...], g_ref[...]) * (1.0 + sc_ref[...]) + sh_ref[...]
        h_scr[...] = h.astype(BF16)

    o_ref[...] = _dot(h_scr[...], w_ref[...])


def _in_proj(x2d, g, sh, sc, w_in_p, layer, seq_len):
    t, d = x2d.shape
    n = w_in_p.shape[-1]
    shared = sh.shape[0] == 1
    tm = _token_tile(t, seq_len, shared, 1024)
    tn = IN_PROJ_TN
    per_seq = seq_len // tm
    mod_map = (lambda i, j: (0, 0, 0)) if shared else (lambda i, j: (i // per_seq, 0, 0))
    return pl.pallas_call(
        _in_proj_kernel,
        grid=(t // tm, n // tn),
        in_specs=[pl.BlockSpec((tm, d), lambda i, j: (i, 0)),
                  pl.BlockSpec((None, 1, d), lambda i, j: (layer, 0, 0)),
                  pl.BlockSpec((None, 1, d), mod_map),
                  pl.BlockSpec((None, 1, d), mod_map),
                  pl.BlockSpec((None, d, tn), lambda i, j: (layer, 0, j))],
        out_specs=pl.BlockSpec((tm, tn), lambda i, j: (i, j)),
        out_shape=jax.ShapeDtypeStruct((t, n), F32),
        scratch_shapes=[pltpu.VMEM((tm, d), BF16)],
        compiler_params=_cparams("arbitrary", "arbitrary"),
        name="in_proj",
    )(x2d, g, sh, sc, w_in_p)


def _gmlp_kernel(u_ref, v_ref, g_ref, ws_ref, bs_ref, o_ref):
    v = _rms(jax.nn.gelu(v_ref[...]), g_ref[...]).astype(BF16)
    u = jax.nn.gelu(u_ref[...])
    tm, gw = u.shape
    hd = gw // A_H
    for c in range(tm // A_CHUNK):
        rows = slice(c * A_CHUNK, (c + 1) * A_CHUNK)
        for h in range(A_H):
            cols = slice(h * hd, (h + 1) * hd)
            r = _dot(ws_ref[h], v[rows, cols]) + bs_ref[h]
            o_ref[rows, cols] = (u[rows, cols] * r).astype(BF16)


def _gmlp(proj2d, g, ws_b, bs_full, layer, seq_len):
    t = proj2d.shape[0]
    gw = g.shape[-1]
    tm = _token_tile(t, seq_len, True, 512)
    return pl.pallas_call(
        _gmlp_kernel,
        grid=(t // tm,),
        in_specs=[pl.BlockSpec((tm, gw), lambda i: (i, 0)),
                  pl.BlockSpec((tm, gw), lambda i: (i, 1)),
                  pl.BlockSpec((None, 1, gw), lambda i: (layer, 0, 0)),
                  pl.BlockSpec((None, A_H, A_CHUNK, A_CHUNK), lambda i: (layer, 0, 0, 0)),
                  pl.BlockSpec((None, A_H, A_CHUNK, gw // A_H), lambda i: (layer, 0, 0, 0))],
        out_specs=pl.BlockSpec((tm, gw), lambda i: (i, 0)),
        out_shape=jax.ShapeDtypeStruct((t, gw), BF16),
        compiler_params=_cparams("arbitrary"),
        name="gmlp",
    )(proj2d, proj2d, g, ws_b, bs_full)


HG_LEVELS = int(np.log2(CHUNK))


def _hgrn_masks():
    c = CHUNK
    mk = np.zeros((HG_LEVELS, c, c), np.float32)
    sg = np.zeros((HG_LEVELS, c, LANE), np.float32)
    u = np.arange(c)
    for lv in range(HG_LEVELS):
        blk = u // (c >> lv)
        mk[lv] = blk[:, None] == blk[None, :]
        later = (u % (c >> lv)) >= (c >> (lv + 1))
        sg[lv] = np.where(later, np.log2(np.e), -np.log2(np.e))[:, None]
    return mk, sg


def _cumsum_rows(x, reverse, row8):
    n = x.shape[0] // SUBLANE
    tiles = []
    for i in range(n):
        t = x[i * SUBLANE:(i + 1) * SUBLANE]
        d = 1
        while d < SUBLANE:
            shift = SUBLANE - d if reverse else d
            ok = (row8 < SUBLANE - d) if reverse else (row8 >= d)
            t = t + jnp.where(ok, pltpu.roll(t, shift, 0), 0.0)
            d *= 2
        tiles.append(t)
    carry = None
    for i in (range(n - 1, -1, -1) if reverse else range(n)):
        if carry is not None:
            tiles[i] = tiles[i] + carry
        carry = tiles[i][0:1] if reverse else tiles[i][SUBLANE - 1:SUBLANE]
    return jnp.concatenate(tiles, axis=0)


def _level_ref(cum, m, d, rowc, row8):
    c, hd = cum.shape
    off = m - 1 if d == 0 else m
    if m == 1:
        odd = (rowc % 2) == 1
        if d == 0:
            return jnp.where(odd, pltpu.roll(cum, 1, 0), cum)
        return jnp.where(odd, cum, pltpu.roll(cum, c - 1, 0))
    bcast = lambda r, n: jnp.broadcast_to(cum[r:r + 1, :], (n, hd))
    if 2 * m < SUBLANE:
        per = SUBLANE // (2 * m)
        pieces = []
        for v in range(c // SUBLANE):
            piece = bcast(v * SUBLANE + off, SUBLANE)
            for k in range(1, per):
                piece = jnp.where(row8 >= k * 2 * m, bcast(v * SUBLANE + k * 2 * m + off, SUBLANE), piece)
            pieces.append(piece)
        return jnp.concatenate(pieces, axis=0)
    return jnp.concatenate([bcast(k * 2 * m + off, 2 * m) for k in range(c // (2 * m))], axis=0)


def _hgrn_kernel(q_ref, ff_ref, fb_ref, i_ref, gt_ref, lb_ref, ng_ref, s0_ref, m_ref, sg_ref,
                 y_ref, sn_ref, o_scr, qd_scr, kv_scr, dec_scr, st_scr, *, seq_len):
    c = CHUNK
    n_chunks = seq_len // c
    hd = q_ref.shape[-1]
    lb = lb_ref[...]
    rowc = lax.broadcasted_iota(jnp.int32, (c, hd), 0)
    row8 = lax.broadcasted_iota(jnp.int32, (SUBLANE, hd), 0)
    f_refs = (ff_ref, fb_ref)

    def intra(i, carry):
        rows = pl.ds(pl.multiple_of(i * c, c), c)
        q = _silu(q_ref[rows, :])
        v = i_ref[rows, :]
        vb = v.astype(BF16)
        kk, cum, tot = [], [], []
        for d in range(2):
            f = lb + (1.0 - lb) * _sigmoid_rel(f_refs[d][rows, :])
            kk.append(1.0 - f)
            cum.append(_cumsum_rows(jnp.log(f), d == 1, row8))
            tot.append(cum[d][c - 1:c] if d == 0 else cum[d][0:1])
        scores = jnp.zeros((c, c), F32)
        for lv in range(HG_LEVELS):
            m = c >> (lv + 1)
            later = (rowc % (2 * m)) >= m
            sg = sg_ref[lv]
            el = [jnp.exp2((cum[0] - _level_ref(cum[0], m, 0, rowc, row8)) * sg),
                  jnp.exp2((_level_ref(cum[1], m, 1, rowc, row8) - cum[1]) * sg)]
            ql = jnp.concatenate([jnp.where(later, q * el[0], 0.0).astype(BF16),
                                  jnp.where(later, 0.0, q * el[1]).astype(BF16)], axis=1)
            kl = jnp.concatenate([jnp.where(later, 0.0, kk[0] * el[0]).astype(BF16),
                                  jnp.where(later, kk[1] * el[1], 0.0).astype(BF16)], axis=1)
            scores = scores + _dot_nt(ql, kl) * m_ref[lv]
        o_scr[rows, :] = (_dot(scores.astype(BF16), vb)
                          + jnp.sum(q * (kk[0] + kk[1]), axis=-1, keepdims=True) * v)
        kd = [(kk[d] * jnp.exp(tot[d] - cum[d])).astype(BF16) for d in range(2)]
        kv_scr[i] = _dot_tn(vb, jnp.concatenate(kd, axis=1))
        for d in range(2):
            qd_scr[d, rows, :] = (q * jnp.exp(cum[d])).astype(BF16)
            dec_scr[i, :, d * hd:(d + 1) * hd] = jnp.exp(tot[d])
        return carry

    lax.fori_loop(0, n_chunks, intra, 0, unroll=2)

    st_scr[0] = s0_ref[0].T
    st_scr[1] = s0_ref[1].T

    def inter(i, carry):
        for d in range(2):
            j = i if d == 0 else n_chunks - 1 - i
            rows = pl.ds(pl.multiple_of(j * c, c), c)
            st = st_scr[d]
            o_scr[rows, :] += _dot_nt(qd_scr[d, rows, :], st.astype(BF16))
            st_scr[d] = st * dec_scr[j, :, d * hd:(d + 1) * hd] + kv_scr[j, :, d * hd:(d + 1) * hd]
        return carry

    lax.fori_loop(0, n_chunks, inter, 0, unroll=min(4, n_chunks))
    sn_ref[0] = st_scr[0].T
    sn_ref[1] = st_scr[1].T

    def finish(i, carry):
        rows = pl.ds(pl.multiple_of(i * c, c), c)
        y_ref[rows, :] = (_rms(o_scr[rows, :], ng_ref[...]) * _silu(gt_ref[rows, :])).astype(BF16)
        return carry

    lax.fori_loop(0, n_chunks, finish, 0, unroll=min(4, n_chunks))


def _hgrn(proj3d, lb, ng, s0, m_tab, layer, gw_blocks):
    b, seq_len, _ = proj3d.shape
    hd = LANE
    n_chunks = seq_len // CHUNK
    base = 2 * gw_blocks
    col = lambda k: (lambda bi, h: (bi, 0, base + k * gw_blocks + h))
    specs = [pl.BlockSpec((None, seq_len, hd), col(k)) for k in range(5)]
    return pl.pallas_call(
        functools.partial(_hgrn_kernel, seq_len=seq_len),
        grid=(b, HG_H),
        in_specs=specs + [
            pl.BlockSpec((None, 1, hd), lambda bi, h: (layer, 0, h)),
            pl.BlockSpec((None, 1, hd), lambda bi, h: (layer, 0, 0)),
            pl.BlockSpec((None, 2, None, hd, hd), lambda bi, h: (bi, 0, h, 0, 0)),
            pl.BlockSpec((HG_LEVELS, CHUNK, CHUNK), lambda bi, h: (0, 0, 0)),
            pl.BlockSpec((HG_LEVELS, CHUNK, hd), lambda bi, h: (0, 0, 0))],
        out_specs=[pl.BlockSpec((None, seq_len, hd), lambda bi, h: (bi, 0, h)),
                   pl.BlockSpec((None, 2, None, hd, hd), lambda bi, h: (bi, 0, h, 0, 0))],
        out_shape=[jax.ShapeDtypeStruct((b, seq_len, HG_H * hd), BF16),
                   jax.ShapeDtypeStruct((b, 2, HG_H, hd, hd), F32)],
        scratch_shapes=[pltpu.VMEM((seq_len, hd), F32), pltpu.VMEM((2, seq_len, hd), BF16),
                        pltpu.VMEM((n_chunks, hd, 2 * hd), F32), pltpu.VMEM((n_chunks, 1, 2 * hd), F32),
                        pltpu.VMEM((2, hd, hd), F32)],
        compiler_params=_cparams("arbitrary", "arbitrary"),
        name="hgrn",
    )(proj3d, proj3d, proj3d, proj3d, proj3d, lb, ng, s0, *m_tab)


SCAN_ROWS = 64


def _scan_block(a_blk, b_blk, carry, row, reverse):
    n_tiles = a_blk.shape[0] // SUBLANE
    outs = [None] * n_tiles
    order = range(n_tiles - 1, -1, -1) if reverse else range(n_tiles)
    for i in order:
        a = a_blk[i * SUBLANE:(i + 1) * SUBLANE]
        bb = b_blk[i * SUBLANE:(i + 1) * SUBLANE]
        d = 1
        while d < SUBLANE:
            shift = SUBLANE - d if reverse else d
            ok = (row < SUBLANE - d) if reverse else (row >= d)
            a_s = jnp.where(ok, pltpu.roll(a, shift, 0), 1.0)
            b_s = jnp.where(ok, pltpu.roll(bb, shift, 0), 0.0)
            bb = a * b_s + bb
            a = a * a_s
            d *= 2
        h = bb + a * carry
        carry = h[0:1] if reverse else h[SUBLANE - 1:SUBLANE]
        outs[i] = h
    return jnp.concatenate(outs, axis=0), carry


def _prep_rows(seq_len, width):
    rows = max(width, min(256, seq_len))
    assert rows % width == 0 and seq_len % rows == 0
    return rows


def _lru_kernel(cx_ref, cg_ref, cw_ref, cb_ref, w_ref, bias_ref, lam_ref, h0_ref,
                y_ref, hn_ref, a_scr, b_scr, h_scr, *, seq_len, width):
    lanes = cx_ref.shape[-1]
    pr = _prep_rows(seq_len, width)
    pos = lax.broadcasted_iota(jnp.int32, (pr, lanes), 0) % width
    lsl = LRU_C * _log_sigmoid(lam_ref[...])

    def prep(i, carry):
        rows = pl.ds(pl.multiple_of(i * pr, pr), pr)
        xc = _conv_block(cx_ref[rows, :], cw_ref[...], cb_ref[...], pos, width)
        gates = _sigmoid(_dot(xc.astype(BF16), w_ref[...]) + bias_ref[...])
        for d in range(2):
            r = gates[:, (2 * d) * lanes:(2 * d + 1) * lanes]
            ig = gates[:, (2 * d + 1) * lanes:(2 * d + 2) * lanes]
            log_a = r * lsl[d:d + 1]
            a = jnp.exp(log_a)
            w = -jnp.tanh(log_a) * (a * a + 1.0)
            root = jnp.where(w > 0.0, w * lax.rsqrt(w), 0.0)
            bx = root * (ig * xc)
            a_scr[d, rows, :] = a
            b_scr[d, rows, :] = bx
        return carry

    lax.fori_loop(0, seq_len // pr, prep, 0)

    row = lax.broadcasted_iota(jnp.int32, (SUBLANE, lanes), 0)
    n_blocks = seq_len // SCAN_ROWS

    def scan(i, carry):
        cf, cb = carry
        rf = pl.ds(pl.multiple_of(i * SCAN_ROWS, SCAN_ROWS), SCAN_ROWS)
        rb = pl.ds(pl.multiple_of((n_blocks - 1 - i) * SCAN_ROWS, SCAN_ROWS), SCAN_ROWS)
        hf, cf = _scan_block(a_scr[0, rf, :], b_scr[0, rf, :], cf, row, False)
        hb, cb = _scan_block(a_scr[1, rb, :], b_scr[1, rb, :], cb, row, True)
        h_scr[0, rf, :] = hf
        h_scr[1, rb, :] = hb
        return cf, cb

    cf, cb = lax.fori_loop(0, n_blocks, scan, (h0_ref[0:1, :], h0_ref[1:2, :]))
    hn_ref[0:1, :] = cf
    hn_ref[1:2, :] = cb

    def finish(i, carry):
        rows = pl.ds(pl.multiple_of(i * pr, pr), pr)
        y_ref[rows, :] = ((h_scr[0, rows, :] + h_scr[1, rows, :]) * jax.nn.gelu(cg_ref[rows, :])).astype(BF16)
        return carry

    lax.fori_loop(0, seq_len // pr, finish, 0)


def _lru(proj3d, cw, cb, wcat, bcat, lam, h0, layer, width, gw_blocks):
    b, seq_len, _ = proj3d.shape
    n_grp = gw_blocks
    base = 7 * gw_blocks
    return pl.pallas_call(
        functools.partial(_lru_kernel, seq_len=seq_len, width=width),
        grid=(b, n_grp),
        in_specs=[pl.BlockSpec((None, seq_len, LANE), lambda bi, j: (bi, 0, base + j)),
                  pl.BlockSpec((None, seq_len, LANE), lambda bi, j: (bi, 0, base + gw_blocks + j)),
                  pl.BlockSpec((None, CONV_K, LANE), lambda bi, j: (layer, 0, j)),
                  pl.BlockSpec((None, 1, LANE), lambda bi, j: (layer, 0, j)),
                  pl.BlockSpec((None, None, LANE, 4 * LANE), lambda bi, j: (layer, j, 0, 0)),
                  pl.BlockSpec((None, None, 1, 4 * LANE), lambda bi, j: (layer, j, 0, 0)),
                  pl.BlockSpec((None, 2, LANE), lambda bi, j: (layer, 0, j)),
                  pl.BlockSpec((None, 2, LANE), lambda bi, j: (bi, 0, j))],
        out_specs=[pl.BlockSpec((None, seq_len, LANE), lambda bi, j: (bi, 0, j)),
                   pl.BlockSpec((None, 2, LANE), lambda bi, j: (bi, 0, j))],
        out_shape=[jax.ShapeDtypeStruct((b, seq_len, n_grp * LANE), BF16),
                   jax.ShapeDtypeStruct((b, 2, n_grp * LANE), F32)],
        scratch_shapes=[pltpu.VMEM((2, seq_len, LANE), F32), pltpu.VMEM((2, seq_len, LANE), F32),
                        pltpu.VMEM((2, seq_len, LANE), F32)],
        compiler_params=_cparams("arbitrary", "arbitrary"),
        name="rglru",
    )(proj3d, proj3d, cw, cb, wcat, bcat, lam, h0)


def _ssd_kernel(z_ref, x_ref, bm_ref, cm_ref, dt_ref,
                cwx_ref, cwb_ref, cwc_ref, cbx_ref, cbb_ref, cbc_ref,
                dtb_ref, alog_ref, dd_ref, ng_ref, tri_ref, h0_ref,
                y_ref, hn_ref, xs_scr, bs_scr, cs_scr, ya_scr, st_scr, *, seq_len, width):
    c = CHUNK
    n_chunks = seq_len // c
    gp = x_ref.shape[-1]
    n_hg = SSD_H // SSD_G
    hp = gp // n_hg
    pr = _prep_rows(seq_len, width)
    pos_x = lax.broadcasted_iota(jnp.int32, (pr, gp), 0) % width
    pos_n = lax.broadcasted_iota(jnp.int32, (pr, SSD_N), 0) % width

    def prep(i, carry):
        rows = pl.ds(pl.multiple_of(i * pr, pr), pr)
        xs_scr[rows, :] = _silu(_conv_block(x_ref[rows, :], cwx_ref[...], cbx_ref[...], pos_x, width))
        bs_scr[rows, :] = _silu(_conv_block(bm_ref[rows, :], cwb_ref[...], cbb_ref[...], pos_n, width))
        cs_scr[rows, :] = _silu(_conv_block(cm_ref[rows, :], cwc_ref[...], cbc_ref[...], pos_n, width))
        return carry

    lax.fori_loop(0, seq_len // pr, prep, 0)

    ri = lax.broadcasted_iota(jnp.int32, (c, c), 0)
    ci = lax.broadcasted_iota(jnp.int32, (c, c), 1)
    assert LANE == 2 * hp and SSD_G == 2
    lane_s = lax.broadcasted_iota(jnp.int32, (1, LANE), 1)
    lane_head = lax.broadcasted_iota(jnp.int32, (1, gp), 1) // hp

    grp = pl.program_id(1)

    def chunk_step(d, r0):
        rows = pl.ds(r0, c)
        dt_c = _softplus(dt_ref[rows, :] + dtb_ref[...])
        cum_c = _dot01(tri_ref[d], -jnp.exp(alog_ref[...]) * dt_c)
        def wide(x, h):
            k = d * SSD_H + h
            col = jnp.where(grp == 0, x[:, k:k + 1], x[:, k + n_hg:k + n_hg + 1])
            return jnp.broadcast_to(col, (c, LANE))

        cum_w = [wide(cum_c, h) for h in range(n_hg)]
        dt_w = [wide(dt_c, h) for h in range(n_hg)]
        first = lane_s < hp
        cum_x = jnp.concatenate([jnp.where(first, cum_w[2 * j], cum_w[2 * j + 1]) for j in range(gp // LANE)], axis=1)
        dt_x = jnp.concatenate([jnp.where(first, dt_w[2 * j], dt_w[2 * j + 1]) for j in range(gp // LANE)], axis=1)
        tot = cum_x[c - 1:c] if d == 0 else cum_x[0:1]
        cum_t = cum_c.T[d * SSD_H:(d + 1) * SSD_H]
        xs = xs_scr[rows, :]
        bm_t = bs_scr[rows, :].T.astype(BF16)
        cm = cs_scr[rows, :].astype(BF16)
        xd = xs * dt_x
        xdb = xd.astype(BF16)
        st = st_scr[d]
        gy = _dot(cm, jnp.concatenate([bm_t, st.astype(BF16)], axis=1))
        g = gy[:, :c]
        causal = (ri >= ci) if d == 0 else (ri <= ci)
        lhs, rhs = [], []
        for h in range(n_hg):
            rw = jnp.where(grp == 0, cum_t[h:h + 1], cum_t[n_hg + h:n_hg + h + 1])
            decay = jnp.exp(jnp.where(causal, cum_w[h] - rw, NEG_BIG))
            lhs.append((g * decay).astype(BF16))
            rhs.append(jnp.where(lane_head == h, xdb, jnp.zeros_like(xdb)))
        y = _dot(jnp.concatenate(lhs, axis=1), jnp.concatenate(rhs, axis=0))
        y = y + gy[:, c:] * jnp.exp(cum_x)
        to_end = jnp.exp(tot - cum_x)
        st_scr[d] = st * jnp.exp(tot) + _dot(bm_t, (xd * to_end).astype(BF16))
        return y

    st_scr[0] = h0_ref[0].reshape(gp, SSD_N).T
    st_scr[1] = h0_ref[1].reshape(gp, SSD_N).T

    def body(i, carry):
        rf = pl.multiple_of(i * c, c)
        rb = pl.multiple_of((n_chunks - 1 - i) * c, c)
        ya_scr[0, pl.ds(rf, c), :] = chunk_step(0, rf)
        ya_scr[1, pl.ds(rb, c), :] = chunk_step(1, rb)
        return carry

    lax.fori_loop(0, n_chunks, body, 0, unroll=2)
    hn_ref[0] = st_scr[0].T.reshape(n_hg, hp, SSD_N)
    hn_ref[1] = st_scr[1].T.reshape(n_hg, hp, SSD_N)

    def finish(i, carry):
        rows = pl.ds(pl.multiple_of(i * c, c), c)
        y = ya_scr[0, rows, :] + ya_scr[1, rows, :] + dd_ref[...] * xs_scr[rows, :]
        y = y * _silu(z_ref[rows, :])
        y_ref[rows, :] = _rms(y, ng_ref[...]).astype(BF16)
        return carry

    lax.fori_loop(0, n_chunks, finish, 0, unroll=min(4, n_chunks))


def _ssd(proj3d, cw, cb, dtb, alog, dd, ng, tri, h0, layer, width, gw_blocks):
    b, seq_len, _ = proj3d.shape
    gw = gw_blocks * LANE
    gp = gw // SSD_G
    n_hg = SSD_H // SSD_G
    hp = gp // n_hg
    xw = gp // LANE
    base_z = 9 * gw_blocks
    base_x = 10 * gw_blocks
    base_b = 11 * gw_blocks
    nb = SSD_N // LANE
    base_dt = base_b + 2 * SSD_G * nb
    return pl.pallas_call(
        functools.partial(_ssd_kernel, seq_len=seq_len, width=width),
        grid=(b, SSD_G),
        in_specs=[pl.BlockSpec((None, seq_len, gp), lambda bi, g: (bi, 0, base_z // xw + g)),
                  pl.BlockSpec((None, seq_len, gp), lambda bi, g: (bi, 0, base_x // xw + g)),
                  pl.BlockSpec((None, seq_len, SSD_N), lambda bi, g: (bi, 0, base_b + g)),
                  pl.BlockSpec((None, seq_len, SSD_N), lambda bi, g: (bi, 0, base_b + SSD_G * nb + g)),
                  pl.BlockSpec((None, seq_len, LANE), lambda bi, g: (bi, 0, base_dt)),
                  pl.BlockSpec((None, CONV_K, gp), lambda bi, g: (layer, 0, g)),
                  pl.BlockSpec((None, CONV_K, SSD_N), lambda bi, g: (layer, 0, gw // SSD_N + g)),
                  pl.BlockSpec((None, CONV_K, SSD_N), lambda bi, g: (layer, 0, gw // SSD_N + SSD_G + g)),
                  pl.BlockSpec((None, 1, gp), lambda bi, g: (layer, 0, g)),
                  pl.BlockSpec((None, 1, SSD_N), lambda bi, g: (layer, 0, gw // SSD_N + g)),
                  pl.BlockSpec((None, 1, SSD_N), lambda bi, g: (layer, 0, gw // SSD_N + SSD_G + g)),
                  pl.BlockSpec((None, 1, LANE), lambda bi, g: (layer, 0, 0)),
                  pl.BlockSpec((None, 1, LANE), lambda bi, g: (layer, 0, 0)),
                  pl.BlockSpec((None, 1, gp), lambda bi, g: (layer, 0, g)),
                  pl.BlockSpec((None, 1, gp), lambda bi, g: (layer, 0, g)),
                  pl.BlockSpec((2, CHUNK, CHUNK), lambda bi, g: (0, 0, 0)),
                  pl.BlockSpec((None, 2, n_hg, hp, SSD_N), lambda bi, g: (bi, 0, g, 0, 0))],
        out_specs=[pl.BlockSpec((None, seq_len, gp), lambda bi, g: (bi, 0, g)),
                   pl.BlockSpec((None, 2, n_hg, hp, SSD_N), lambda bi, g: (bi, 0, g, 0, 0))],
        out_shape=[jax.ShapeDtypeStruct((b, seq_len, gw), BF16),
                   jax.ShapeDtypeStruct((b, 2, SSD_H, hp, SSD_N), F32)],
        scratch_shapes=[pltpu.VMEM((seq_len, gp), F32), pltpu.VMEM((seq_len, SSD_N), F32),
                        pltpu.VMEM((seq_len, SSD_N), F32), pltpu.VMEM((2, seq_len, gp), F32),
                        pltpu.VMEM((2, SSD_N, gp), F32)],
        compiler_params=_cparams("arbitrary", "arbitrary"),
        name="ssd",
    )(proj3d, proj3d, proj3d, proj3d, proj3d, cw, cw, cw, cb, cb, cb,
      dtb, alog, dd, ng, tri, h0)


def _out_proj_kernel(x_ref, ya_ref, yb_ref, yc_ref, yd_ref, gate_ref, w_ref, o_ref):
    gw = ya_ref.shape[-1]
    acc = _dot(ya_ref[...], w_ref[0:gw, :])
    for k, r in enumerate((yb_ref, yc_ref, yd_ref), start=1):
        acc = acc + _dot(r[...], w_ref[k * gw:(k + 1) * gw, :])
    o_ref[...] = x_ref[...] + gate_ref[...] * acc


def _out_proj(x2d, ys, gate, w_out_b, layer, seq_len):
    t, d = x2d.shape
    gw = ys[0].shape[-1]
    shared = gate.shape[0] == 1
    tm = _token_tile(t, seq_len, shared, 512)
    per_seq = seq_len // tm
    mod_map = (lambda i: (0, 0, 0)) if shared else (lambda i: (i // per_seq, 0, 0))
    y_spec = pl.BlockSpec((tm, gw), lambda i: (i, 0))
    return pl.pallas_call(
        _out_proj_kernel,
        grid=(t // tm,),
        in_specs=[pl.BlockSpec((tm, d), lambda i: (i, 0)), y_spec, y_spec, y_spec, y_spec,
                  pl.BlockSpec((None, 1, d), mod_map),
                  pl.BlockSpec((None, N_GROUPS * gw, d), lambda i: (layer, 0, 0))],
        out_specs=pl.BlockSpec((tm, d), lambda i: (i, 0)),
        out_shape=jax.ShapeDtypeStruct((t, d), F32),
        compiler_params=_cparams("arbitrary"),
        name="out_proj",
    )(x2d, *ys, gate, w_out_b)


def _ffn_kernel(x_ref, g_ref, sh_ref, sc_ref, gate_ref, w1_ref, w3_ref, w2_ref, fg_ref, o_ref,
                h_scr, acc_scr, *, final_norm):
    j = pl.program_id(1)

    @pl.when(j == 0)
    def _():
        h = _rms(x_ref[...], g_ref[...]) * (1.0 + sc_ref[...]) + sh_ref[...]
        h_scr[...] = h.astype(BF16)
        acc_scr[...] = jnp.zeros_like(acc_scr)

    h = h_scr[...]
    act = (_silu(_dot(h, w1_ref[...])) * _dot(h, w3_ref[...])).astype(BF16)
    acc_scr[...] += _dot(act, w2_ref[...])

    @pl.when(j == pl.num_programs(1) - 1)
    def _():
        xo = x_ref[...] + gate_ref[...] * acc_scr[...]
        if final_norm:
            xo = _rms(xo, fg_ref[...])
        o_ref[...] = xo


def _ffn(x2d, g, sh, sc, gate, w1_b, w3_b, w2_b, fg, layer, seq_len, final_norm):
    t, d = x2d.shape
    dff = w1_b.shape[-1]
    shared = gate.shape[0] == 1
    tm = _token_tile(t, seq_len, shared, 512)
    tf = 512
    per_seq = seq_len // tm
    mod_map = (lambda i, j: (0, 0, 0)) if shared else (lambda i, j: (i // per_seq, 0, 0))
    return pl.pallas_call(
        functools.partial(_ffn_kernel, final_norm=final_norm),
        grid=(t // tm, dff // tf),
        in_specs=[pl.BlockSpec((tm, d), lambda i, j: (i, 0)),
                  pl.BlockSpec((None, 1, d), lambda i, j: (layer, 0, 0)),
                  pl.BlockSpec((None, 1, d), mod_map),
                  pl.BlockSpec((None, 1, d), mod_map),
                  pl.BlockSpec((None, 1, d), mod_map),
                  pl.BlockSpec((None, d, tf), lambda i, j: (layer, 0, j)),
                  pl.BlockSpec((None, d, tf), lambda i, j: (layer, 0, j)),
                  pl.BlockSpec((None, tf, d), lambda i, j: (layer, j, 0)),
                  pl.BlockSpec((1, d), lambda i, j: (0, 0))],
        out_specs=pl.BlockSpec((tm, d), lambda i, j: (i, 0)),
        out_shape=jax.ShapeDtypeStruct((t, d), F32),
        scratch_shapes=[pltpu.VMEM((tm, d), BF16), pltpu.VMEM((tm, d), F32)],
        compiler_params=_cparams("arbitrary", "arbitrary"),
        name="ffn",
    )(x2d, g, sh, sc, gate, w1_b, w3_b, w2_b, fg)


def _block_diag_pairs(w):
    depth, two, h, hd, _ = w.shape
    w = w.reshape(depth, two, h // 2, 2, hd, hd)
    z = jnp.zeros_like(w[:, :, :, 0])
    top = jnp.concatenate([w[:, :, :, 0], z], axis=-1)
    bot = jnp.concatenate([z, w[:, :, :, 1]], axis=-1)
    return jnp.concatenate([top, bot], axis=-2)


def _head_lanes(v):
    depth = v.shape[0]
    flat = v.reshape(depth, 1, 2 * SSD_H)
    return jnp.pad(flat, ((0, 0), (0, 0), (0, LANE - 2 * SSD_H)))


def kernel(x_prompt, x_sample, state_hgrn, state_rglru, state_ssd, c, c_ctx, w_mod, b_mod, norm1_g, norm2_g, w_in, w_out, gmlp_norm_g, gmlp_ws, gmlp_bs, hgrn_lb, hgrn_norm_g, lru_conv_w, lru_conv_b, lru_wr, lru_br, lru_wi, lru_bi, lru_lambda, ssd_conv_w, ssd_conv_b, ssd_dt_bias, ssd_a_log, ssd_d, ssd_norm_g, ffn_w1, ffn_w3, ffn_w2, final_norm_g):
    depth, d_model, d_in = w_in.shape
    gw = d_model // N_GROUPS
    gwb = gw // LANE
    gp = gw // SSD_G
    assert gw // HG_H == LANE and SSD_N == LANE and gp % LANE == 0
    d_in_pad = -(-d_in // IN_PROJ_TN) * IN_PROJ_TN

    w_in_p = jnp.pad(w_in.astype(BF16), ((0, 0), (0, 0), (0, d_in_pad - d_in)))
    w_out_b = w_out.astype(BF16)
    w1_b, w3_b, w2_b = ffn_w1.astype(BF16), ffn_w3.astype(BF16), ffn_w2.astype(BF16)
    row3 = lambda a: a.reshape(depth, 1, -1)
    ws_b = gmlp_ws.astype(BF16)
    bs_full = jnp.broadcast_to(gmlp_bs[..., None], gmlp_bs.shape + (gw // A_H,))
    lbs = jax.nn.softmax(hgrn_lb.astype(F32), axis=0)
    lb = row3(jnp.cumsum(lbs, axis=0) - lbs[0])
    m_tab = tuple(jnp.asarray(t, F32) for t in _hgrn_masks())
    wr_bd, wi_bd = _block_diag_pairs(lru_wr), _block_diag_pairs(lru_wi)
    wcat = jnp.concatenate([wr_bd[:, 0], wi_bd[:, 0], wr_bd[:, 1], wi_bd[:, 1]], axis=-1).astype(BF16)
    grp = lambda a: a.reshape(depth, 2, gwb, LANE)
    bcat = jnp.concatenate([grp(lru_br)[:, 0], grp(lru_bi)[:, 0], grp(lru_br)[:, 1], grp(lru_bi)[:, 1]],
                           axis=-1).reshape(depth, gwb, 1, 4 * LANE)
    dtb_e, alog_e = _head_lanes(ssd_dt_bias), _head_lanes(ssd_a_log)
    dd_e = row3(jnp.repeat(ssd_d, gw // SSD_H, axis=-1))
    tri = jnp.asarray(np.stack([np.tril(np.ones((CHUNK, CHUNK), np.float32)),
                                np.triu(np.ones((CHUNK, CHUNK), np.float32))]), BF16)
    fg = final_norm_g.reshape(1, d_model)

    n_dec = c.shape[0]
    rows = -(-(n_dec + 1) // SUBLANE) * SUBLANE
    cond = jnp.zeros((rows, d_model), F32).at[:n_dec].set(c).at[n_dec].set(c_ctx)
    mod = _modulation(cond, w_mod, b_mod)

    def run_pass(x, row_lo, row_hi, width, st_hg, st_lru, st_ssd):
        b, seq_len, _ = x.shape
        x2d = x.reshape(b * seq_len, d_model)
        s_hg, s_lru, s_ssd = [], [], []
        for l in range(depth):
            m = mod[l, row_lo:row_hi].reshape(row_hi - row_lo, 1, 6, d_model)
            sh1, sc1, g1, sh2, sc2, g2 = (m[:, :, k] for k in range(6))
            proj2d = _in_proj(x2d, row3(norm1_g), sh1, sc1, w_in_p, l, seq_len)
            proj3d = proj2d.reshape(b, seq_len, d_in_pad)
            ya = _gmlp(proj2d, row3(gmlp_norm_g), ws_b, bs_full, l, seq_len)
            yb, hg = _hgrn(proj3d, lb, row3(hgrn_norm_g), st_hg[l], m_tab, l, gwb)
            yc, lr = _lru(proj3d, lru_conv_w, row3(lru_conv_b), wcat, bcat, lru_lambda, st_lru[l], l, width, gwb)
            yd, sd = _ssd(proj3d, ssd_conv_w, row3(ssd_conv_b), dtb_e, alog_e,
                          dd_e, row3(ssd_norm_g), tri, st_ssd[l], l, width, gwb)
            ys = [ya, yb.reshape(-1, gw), yc.reshape(-1, gw), yd.reshape(-1, gw)]
            x2d = _out_proj(x2d, ys, g1, w_out_b, l, seq_len)
            x2d = _ffn(x2d, row3(norm2_g), sh2, sc2, g2, w1_b, w3_b, w2_b, fg, l, seq_len, l == depth - 1)
            s_hg.append(hg)
            s_lru.append(lr)
            s_ssd.append(sd)
        return x2d.reshape(b, seq_len, d_model), s_hg, s_lru, s_ssd

    bp, seq, _ = x_prompt.shape
    hd = gw // HG_H
    z_hg = [jnp.zeros((bp, 2, HG_H, hd, hd), F32)] * depth
    z_lru = [jnp.zeros((bp, 2, gw), F32)] * depth
    z_ssd = [jnp.zeros((bp, 2, SSD_H, gw // SSD_H, SSD_N), F32)] * depth
    y_prompt, s_hg, s_lru, s_ssd = run_pass(x_prompt, n_dec, n_dec + 1, seq, z_hg, z_lru, z_ssd)

    st_hg = [state_hgrn[:, l] for l in range(depth)]
    st_lru = [state_rglru[:, l] for l in range(depth)]
    st_ssd = [state_ssd[:, l] for l in range(depth)]
    y_sample, _, _, _ = run_pass(x_sample, 0, n_dec, GRID_W, st_hg, st_lru, st_ssd)

    return (y_prompt, y_sample, jnp.stack(s_hg, axis=1), jnp.stack(s_lru, axis=1), jnp.stack(s_ssd, axis=1))
```

```python
import functools

import numpy as np
import jax
import jax.numpy as jnp
from jax import lax
from jax.experimental import pallas as pl
from jax.experimental.pallas import tpu as pltpu

F32 = jnp.float32
BF16 = jnp.bfloat16

EPS = 1e-6
N_GROUPS = 4
GRID_W = 64
A_CHUNK = 128
A_H = 4
HG_H = 4
LRU_H = 8
LRU_C = 8.0
CONV_K = 4
CONV_LEFT = 2
SSD_H = 8
SSD_G = 2
SSD_N = 128
LANE = 128
SUBLANE = 8
CHUNK = 128
MXU_N = 256
IN_PROJ_TN = 5 * MXU_N
VMEM_LIMIT = 56 * 1024 * 1024
NEG_BIG = -1e30


def _dot(a, b):
    return jnp.dot(a, b, preferred_element_type=F32)


def _dot_nt(a, b):
    return lax.dot_general(a, b, (((1,), (1,)), ((), ())), preferred_element_type=F32)


def _dot_tn(a, b):
    return lax.dot_general(a, b, (((0,), (0,)), ((), ())), preferred_element_type=F32)


def _split_terms(x, terms):
    hi = x.astype(BF16).astype(F32)
    r = x - hi
    if terms == 2:
        return [hi, r]
    mid = r.astype(BF16).astype(F32)
    return [hi, mid, r - mid]


def _dot01(m01, x, terms=3):
    return sum(_dot(m01, t.astype(BF16)) for t in _split_terms(x, terms))


def _sigmoid(x):
    return 0.5 * jnp.tanh(0.5 * x) + 0.5


def _sigmoid_rel(x):
    return jnp.exp(jnp.minimum(x, 0.0) - jnp.log(1.0 + jnp.exp(-jnp.abs(x))))


def _silu(x):
    return x * _sigmoid(x)


def _softplus(x):
    return jnp.maximum(x, 0.0) + jnp.log1p(jnp.exp(-jnp.abs(x)))


def _log_sigmoid(x):
    return jnp.minimum(x, 0.0) - jnp.log1p(jnp.exp(-jnp.abs(x)))


def _rms(x, g):
    return x * lax.rsqrt(jnp.mean(x * x, axis=-1, keepdims=True) + EPS) * g


def _conv_block(x, w, b, pos, width):
    n = x.shape[0]
    y = b + w[CONV_LEFT:CONV_LEFT + 1] * x
    for j in range(CONV_K):
        s = j - CONV_LEFT
        if s == 0:
            continue
        xs = pltpu.roll(x, (-s) % n, 0)
        ok = (pos + s >= 0) & (pos + s < width)
        y = y + w[j:j + 1] * jnp.where(ok, xs, 0.0)
    return y


def _cparams(*sem):
    return pltpu.CompilerParams(dimension_semantics=sem, vmem_limit_bytes=VMEM_LIMIT)


def _token_tile(t, seq_len, shared, cap):
    tm = min(cap, t if shared else seq_len)
    assert t % tm == 0 and (shared or seq_len % tm == 0)
    return tm


def _mod_kernel(c_ref, w_ref, b_ref, o_ref):
    s = _silu(c_ref[...]).astype(BF16)
    o_ref[...] = _dot(s, w_ref[...].astype(BF16)) + b_ref[...]


def _modulation(cond, w_mod, b_mod):
    depth, d, n = w_mod.shape
    rows = cond.shape[0]
    tn = 1024
    return pl.pallas_call(
        _mod_kernel,
        grid=(depth, n // tn),
        in_specs=[pl.BlockSpec((rows, d), lambda l, j: (0, 0)),
                  pl.BlockSpec((None, d, tn), lambda l, j: (l, 0, j)),
                  pl.BlockSpec((None, 1, tn), lambda l, j: (l, 0, j))],
        out_specs=pl.BlockSpec((None, rows, tn), lambda l, j: (l, 0, j)),
        out_shape=jax.ShapeDtypeStruct((depth, rows, n), F32),
        compiler_params=_cparams("arbitrary", "arbitrary"),
        name="modulation",
    )(cond, w_mod, b_mod.reshape(depth, 1, n))


def _in_proj_kernel(x_ref, g_ref, sh_ref, sc_ref, w_ref, o_ref, h_scr):
    @pl.when(pl.program_id(1) == 0)
    def _():
        h = _rms(x_ref[...], g_ref[...]) * (1.0 + sc_ref[...]) + sh_ref[...]
        h_scr[...] = h.astype(BF16)

    o_ref[...] = _dot(h_scr[...], w_ref[...])


def _in_proj(x2d, g, sh, sc, w_in_p, layer, seq_len):
    t, d = x2d.shape
    n = w_in_p.shape[-1]
    shared = sh.shape[0] == 1
    tm = _token_tile(t, seq_len, shared, 1024)
    tn = IN_PROJ_TN
    per_seq = seq_len // tm
    mod_map = (lambda i, j: (0, 0, 0)) if shared else (lambda i, j: (i // per_seq, 0, 0))
    return pl.pallas_call(
        _in_proj_kernel,
        grid=(t // tm, n // tn),
        in_specs=[pl.BlockSpec((tm, d), lambda i, j: (i, 0)),
                  pl.BlockSpec((None, 1, d), lambda i, j: (layer, 0, 0)),
                  pl.BlockSpec((None, 1, d), mod_map),
                  pl.BlockSpec((None, 1, d), mod_map),
                  pl.BlockSpec((None, d, tn), lambda i, j: (layer, 0, j))],
        out_specs=pl.BlockSpec((tm, tn), lambda i, j: (i, j)),
        out_shape=jax.ShapeDtypeStruct((t, n), F32),
        scratch_shapes=[pltpu.VMEM((tm, d), BF16)],
        compiler_params=_cparams("arbitrary", "arbitrary"),
        name="in_proj",
    )(x2d, g, sh, sc, w_in_p)


def _gmlp_kernel(u_ref, v_ref, g_ref, ws_ref, bs_ref, o_ref):
    v = _rms(jax.nn.gelu(v_ref[...]), g_ref[...]).astype(BF16)
    u = jax.nn.gelu(u_ref[...])
    tm, gw = u.shape
    hd = gw // A_H
    for c in range(tm // A_CHUNK):
        rows = slice(c * A_CHUNK, (c + 1) * A_CHUNK)
        for h in range(A_H):
            cols = slice(h * hd, (h + 1) * hd)
            r = _dot(ws_ref[h], v[rows, cols]) + bs_ref[h]
            o_ref[rows, cols] = (u[rows, cols] * r).astype(BF16)


def _gmlp(proj2d, g, ws_b, bs_full, layer, seq_len):
    t = proj2d.shape[0]
    gw = g.shape[-1]
    tm = _token_tile(t, seq_len, True, 512)
    return pl.pallas_call(
        _gmlp_kernel,
        grid=(t // tm,),
        in_specs=[pl.BlockSpec((tm, gw), lambda i: (i, 0)),
                  pl.BlockSpec((tm, gw), lambda i: (i, 1)),
                  pl.BlockSpec((None, 1, gw), lambda i: (layer, 0, 0)),
                  pl.BlockSpec((None, A_H, A_CHUNK, A_CHUNK), lambda i: (layer, 0, 0, 0)),
                  pl.BlockSpec((None, A_H, A_CHUNK, gw // A_H), lambda i: (layer, 0, 0, 0))],
        out_specs=pl.BlockSpec((tm, gw), lambda i: (i, 0)),
        out_shape=jax.ShapeDtypeStruct((t, gw), BF16),
        compiler_params=_cparams("arbitrary"),
        name="gmlp",
    )(proj2d, proj2d, g, ws_b, bs_full)


HG_LEVELS = int(np.log2(CHUNK))


def _hgrn_masks():
    c = CHUNK
    mk = np.zeros((HG_LEVELS, c, c), np.float32)
    sg = np.zeros((HG_LEVELS, c, LANE), np.float32)
    u = np.arange(c)
    for lv in range(HG_LEVELS):
        blk = u // (c >> lv)
        later = (u % (c >> lv)) >= (c >> (lv + 1))
        mk[lv] = blk[:, None] == blk[None, :]
        sg[lv] = np.where(later, np.log2(np.e), -np.log2(np.e))[:, None]
    return mk, sg


def _cumsum_rows(x, reverse, row8):
    n = x.shape[0] // SUBLANE
    tiles = []
    for i in range(n):
        t = x[i * SUBLANE:(i + 1) * SUBLANE]
        d = 1
        while d < SUBLANE:
            shift = SUBLANE - d if reverse else d
            ok = (row8 < SUBLANE - d) if reverse else (row8 >= d)
            t = t + jnp.where(ok, pltpu.roll(t, shift, 0), 0.0)
            d *= 2
        tiles.append(t)
    carry = None
    for i in (range(n - 1, -1, -1) if reverse else range(n)):
        if carry is not None:
            tiles[i] = tiles[i] + carry
        carry = tiles[i][0:1] if reverse else tiles[i][SUBLANE - 1:SUBLANE]
    return jnp.concatenate(tiles, axis=0)


def _level_ref(cum, m, d, rowc, row8):
    c, hd = cum.shape
    off = m - 1 if d == 0 else m
    if m == 1:
        odd = (rowc % 2) == 1
        if d == 0:
            return jnp.where(odd, pltpu.roll(cum, 1, 0), cum)
        return jnp.where(odd, cum, pltpu.roll(cum, c - 1, 0))
    bcast = lambda r, n: jnp.broadcast_to(cum[r:r + 1, :], (n, hd))
    if 2 * m < SUBLANE:
        per = SUBLANE // (2 * m)
        pieces = []
        for v in range(c // SUBLANE):
            piece = bcast(v * SUBLANE + off, SUBLANE)
            for k in range(1, per):
                piece = jnp.where(row8 >= k * 2 * m, bcast(v * SUBLANE + k * 2 * m + off, SUBLANE), piece)
            pieces.append(piece)
        return jnp.concatenate(pieces, axis=0)
    return jnp.concatenate([bcast(k * 2 * m + off, 2 * m) for k in range(c // (2 * m))], axis=0)


def _hgrn_kernel(q_ref, ff_ref, fb_ref, i_ref, gt_ref, lb_ref, ng_ref, s0_ref, m_ref, sg_ref,
                 y_ref, sn_ref, o_scr, qd_scr, kv_scr, dec_scr, st_scr, *, seq_len):
    c = CHUNK
    n_chunks = seq_len // c
    hd = q_ref.shape[-1]
    lb = lb_ref[...]
    rowc = lax.broadcasted_iota(jnp.int32, (c, hd), 0)
    row8 = lax.broadcasted_iota(jnp.int32, (SUBLANE, hd), 0)
    f_refs = (ff_ref, fb_ref)

    def intra(i, carry):
        rows = pl.ds(pl.multiple_of(i * c, c), c)
        q = _silu(q_ref[rows, :])
        v = i_ref[rows, :]
        vb = v.astype(BF16)
        kk, cum, tot = [], [], []
        for d in range(2):
            f = lb + (1.0 - lb) * _sigmoid_rel(f_refs[d][rows, :])
            kk.append(1.0 - f)
            cum.append(_cumsum_rows(jnp.log(f), d == 1, row8))
            tot.append(cum[d][c - 1:c] if d == 0 else cum[d][0:1])
        scores = jnp.zeros((c, c), F32)
        for lv in range(HG_LEVELS):
            m = c >> (lv + 1)
            later = (rowc % (2 * m)) >= m
            sg = sg_ref[lv]
            el = [jnp.exp2((cum[0] - _level_ref(cum[0], m, 0, rowc, row8)) * sg),
                  jnp.exp2((_level_ref(cum[1], m, 1, rowc, row8) - cum[1]) * sg)]
            ql = jnp.concatenate([jnp.where(later, q * el[0], 0.0).astype(BF16),
                                  jnp.where(later, 0.0, q * el[1]).astype(BF16)], axis=1)
            kl = jnp.concatenate([jnp.where(later, 0.0, kk[0] * el[0]).astype(BF16),
                                  jnp.where(later, kk[1] * el[1], 0.0).astype(BF16)], axis=1)
            scores = scores + _dot_nt(ql, kl) * m_ref[lv]
        o_scr[rows, :] = (_dot(scores.astype(BF16), vb)
                          + jnp.sum(q * (kk[0] + kk[1]), axis=-1, keepdims=True) * v)
        kd = [(kk[d] * jnp.exp(tot[d] - cum[d])).astype(BF16) for d in range(2)]
        kv_scr[i] = _dot_tn(vb, jnp.concatenate(kd, axis=1))
        for d in range(2):
            qd_scr[d, rows, :] = (q * jnp.exp(cum[d])).astype(BF16)
            dec_scr[i, :, d * hd:(d + 1) * hd] = jnp.exp(tot[d])
        return carry

    lax.fori_loop(0, n_chunks, intra, 0, unroll=2)

    st_scr[0] = s0_ref[0].T
    st_scr[1] = s0_ref[1].T

    def inter(i, carry):
        for d in range(2):
            j = i if d == 0 else n_chunks - 1 - i
            rows = pl.ds(pl.multiple_of(j * c, c), c)
            st = st_scr[d]
            o_scr[rows, :] += _dot_nt(qd_scr[d, rows, :], st.astype(BF16))
            st_scr[d] = st * dec_scr[j, :, d * hd:(d + 1) * hd] + kv_scr[j, :, d * hd:(d + 1) * hd]
        return carry

    lax.fori_loop(0, n_chunks, inter, 0, unroll=min(4, n_chunks))
    sn_ref[0] = st_scr[0].T
    sn_ref[1] = st_scr[1].T

    def finish(i, carry):
        rows = pl.ds(pl.multiple_of(i * c, c), c)
        y_ref[rows, :] = (_rms(o_scr[rows, :], ng_ref[...]) * _silu(gt_ref[rows, :])).astype(BF16)
        return carry

    lax.fori_loop(0, n_chunks, finish, 0, unroll=min(4, n_chunks))


def _hgrn(proj3d, lb, ng, s0, m_tab, layer, gw_blocks):
    b, seq_len, _ = proj3d.shape
    hd = LANE
    n_chunks = seq_len // CHUNK
    base = 2 * gw_blocks
    col = lambda k: (lambda bi, h: (bi, 0, base + k * gw_blocks + h))
    specs = [pl.BlockSpec((None, seq_len, hd), col(k)) for k in range(5)]
    return pl.pallas_call(
        functools.partial(_hgrn_kernel, seq_len=seq_len),
        grid=(b, HG_H),
        in_specs=specs + [
            pl.BlockSpec((None, 1, hd), lambda bi, h: (layer, 0, h)),
            pl.BlockSpec((None, 1, hd), lambda bi, h: (layer, 0, 0)),
            pl.BlockSpec((None, 2, None, hd, hd), lambda bi, h: (bi, 0, h, 0, 0)),
            pl.BlockSpec((HG_LEVELS, CHUNK, CHUNK), lambda bi, h: (0, 0, 0)),
            pl.BlockSpec((HG_LEVELS, CHUNK, hd), lambda bi, h: (0, 0, 0))],
        out_specs=[pl.BlockSpec((None, seq_len, hd), lambda bi, h: (bi, 0, h)),
                   pl.BlockSpec((None, 2, None, hd, hd), lambda bi, h: (bi, 0, h, 0, 0))],
        out_shape=[jax.ShapeDtypeStruct((b, seq_len, HG_H * hd), BF16),
                   jax.ShapeDtypeStruct((b, 2, HG_H, hd, hd), F32)],
        scratch_shapes=[pltpu.VMEM((seq_len, hd), F32), pltpu.VMEM((2, seq_len, hd), BF16),
                        pltpu.VMEM((n_chunks, hd, 2 * hd), F32), pltpu.VMEM((n_chunks, 1, 2 * hd), F32),
                        pltpu.VMEM((2, hd, hd), F32)],
        compiler_params=_cparams("arbitrary", "arbitrary"),
        name="hgrn",
    )(proj3d, proj3d, proj3d, proj3d, proj3d, lb, ng, s0, *m_tab)


SCAN_ROWS = 64


def _scan_block(a_blk, b_blk, carry, row, reverse):
    n_tiles = a_blk.shape[0] // SUBLANE
    outs = [None] * n_tiles
    order = range(n_tiles - 1, -1, -1) if reverse else range(n_tiles)
    for i in order:
        a = a_blk[i * SUBLANE:(i + 1) * SUBLANE]
        bb = b_blk[i * SUBLANE:(i + 1) * SUBLANE]
        d = 1
        while d < SUBLANE:
            shift = SUBLANE - d if reverse else d
            ok = (row < SUBLANE - d) if reverse else (row >= d)
            a_s = jnp.where(ok, pltpu.roll(a, shift, 0), 1.0)
            b_s = jnp.where(ok, pltpu.roll(bb, shift, 0), 0.0)
            bb = a * b_s + bb
            a = a * a_s
            d *= 2
        h = bb + a * carry
        carry = h[0:1] if reverse else h[SUBLANE - 1:SUBLANE]
        outs[i] = h
    return jnp.concatenate(outs, axis=0), carry


def _prep_rows(seq_len, width):
    rows = max(width, min(256, seq_len))
    assert rows % width == 0 and seq_len % rows == 0
    return rows


def _lru_kernel(cx_ref, cg_ref, cw_ref, cb_ref, w_ref, bias_ref, lam_ref, h0_ref,
                y_ref, hn_ref, a_scr, b_scr, h_scr, *, seq_len, width):
    lanes = cx_ref.shape[-1]
    pr = _prep_rows(seq_len, width)
    pos = lax.broadcasted_iota(jnp.int32, (pr, lanes), 0) % width
    lsl = LRU_C * _log_sigmoid(lam_ref[...])

    def prep(i, carry):
        rows = pl.ds(pl.multiple_of(i * pr, pr), pr)
        xc = _conv_block(cx_ref[rows, :], cw_ref[...], cb_ref[...], pos, width)
        gates = _sigmoid(_dot(xc.astype(BF16), w_ref[...]) + bias_ref[...])
        for d in range(2):
            r = gates[:, (2 * d) * lanes:(2 * d + 1) * lanes]
            ig = gates[:, (2 * d + 1) * lanes:(2 * d + 2) * lanes]
            log_a = r * lsl[d:d + 1]
            a = jnp.exp(log_a)
            w = -jnp.tanh(log_a) * (a * a + 1.0)
            root = jnp.where(w > 0.0, w * lax.rsqrt(w), 0.0)
            bx = root * (ig * xc)
            a_scr[d, rows, :] = a
            b_scr[d, rows, :] = bx
        return carry

    lax.fori_loop(0, seq_len // pr, prep, 0)

    row = lax.broadcasted_iota(jnp.int32, (SUBLANE, lanes), 0)
    n_blocks = seq_len // SCAN_ROWS

    def scan(i, carry):
        cf, cb = carry
        rf = pl.ds(pl.multiple_of(i * SCAN_ROWS, SCAN_ROWS), SCAN_ROWS)
        rb = pl.ds(pl.multiple_of((n_blocks - 1 - i) * SCAN_ROWS, SCAN_ROWS), SCAN_ROWS)
        hf, cf = _scan_block(a_scr[0, rf, :], b_scr[0, rf, :], cf, row, False)
        hb, cb = _scan_block(a_scr[1, rb, :], b_scr[1, rb, :], cb, row, True)
        h_scr[0, rf, :] = hf
        h_scr[1, rb, :] = hb
        return cf, cb

    cf, cb = lax.fori_loop(0, n_blocks, scan, (h0_ref[0:1, :], h0_ref[1:2, :]))
    hn_ref[0:1, :] = cf
    hn_ref[1:2, :] = cb

    def finish(i, carry):
        rows = pl.ds(pl.multiple_of(i * pr, pr), pr)
        y_ref[rows, :] = ((h_scr[0, rows, :] + h_scr[1, rows, :]) * jax.nn.gelu(cg_ref[rows, :])).astype(BF16)
        return carry

    lax.fori_loop(0, seq_len // pr, finish, 0)


def _lru(proj3d, cw, cb, wcat, bcat, lam, h0, layer, width, gw_blocks):
    b, seq_len, _ = proj3d.shape
    n_grp = gw_blocks
    base = 7 * gw_blocks
    return pl.pallas_call(
        functools.partial(_lru_kernel, seq_len=seq_len, width=width),
        grid=(b, n_grp),
        in_specs=[pl.BlockSpec((None, seq_len, LANE), lambda bi, j: (bi, 0, base + j)),
                  pl.BlockSpec((None, seq_len, LANE), lambda bi, j: (bi, 0, base + gw_blocks + j)),
                  pl.BlockSpec((None, CONV_K, LANE), lambda bi, j: (layer, 0, j)),
                  pl.BlockSpec((None, 1, LANE), lambda bi, j: (layer, 0, j)),
                  pl.BlockSpec((None, None, LANE, 4 * LANE), lambda bi, j: (layer, j, 0, 0)),
                  pl.BlockSpec((None, None, 1, 4 * LANE), lambda bi, j: (layer, j, 0, 0)),
                  pl.BlockSpec((None, 2, LANE), lambda bi, j: (layer, 0, j)),
                  pl.BlockSpec((None, 2, LANE), lambda bi, j: (bi, 0, j))],
        out_specs=[pl.BlockSpec((None, seq_len, LANE), lambda bi, j: (bi, 0, j)),
                   pl.BlockSpec((None, 2, LANE), lambda bi, j: (bi, 0, j))],
        out_shape=[jax.ShapeDtypeStruct((b, seq_len, n_grp * LANE), BF16),
                   jax.ShapeDtypeStruct((b, 2, n_grp * LANE), F32)],
        scratch_shapes=[pltpu.VMEM((2, seq_len, LANE), F32), pltpu.VMEM((2, seq_len, LANE), F32),
                        pltpu.VMEM((2, seq_len, LANE), F32)],
        compiler_params=_cparams("arbitrary", "arbitrary"),
        name="rglru",
    )(proj3d, proj3d, cw, cb, wcat, bcat, lam, h0)


def _ssd_kernel(z_ref, x_ref, bm_ref, cm_ref, dt_ref,
                cwx_ref, cwb_ref, cwc_ref, cbx_ref, cbb_ref, cbc_ref,
                dtb_ref, alog_ref, dd_ref, ng_ref, h0_ref,
                y_ref, hn_ref, xs_scr, bs_scr, cs_scr, ya_scr, ecx_scr, inc_scr, dtot_scr, st_scr,
                *, seq_len, width):
    c = CHUNK
    n_chunks = seq_len // c
    gp = x_ref.shape[-1]
    n_hg = SSD_H // SSD_G
    hp = gp // n_hg
    pr = _prep_rows(seq_len, width)
    pos_x = lax.broadcasted_iota(jnp.int32, (pr, gp), 0) % width
    pos_n = lax.broadcasted_iota(jnp.int32, (pr, SSD_N), 0) % width

    def prep(i, carry):
        rows = pl.ds(pl.multiple_of(i * pr, pr), pr)
        xs_scr[rows, :] = _silu(_conv_block(x_ref[rows, :], cwx_ref[...], cbx_ref[...], pos_x, width))
        bs_scr[rows, :] = _silu(_conv_block(bm_ref[rows, :], cwb_ref[...], cbb_ref[...], pos_n, width))
        cs_scr[rows, :] = _silu(_conv_block(cm_ref[rows, :], cwc_ref[...], cbc_ref[...], pos_n, width))
        return carry

    lax.fori_loop(0, seq_len // pr, prep, 0)

    ri = lax.broadcasted_iota(jnp.int32, (c, c), 0)
    ci = lax.broadcasted_iota(jnp.int32, (c, c), 1)
    assert LANE == 2 * hp and SSD_G == 2
    lane_s = lax.broadcasted_iota(jnp.int32, (1, LANE), 1)
    lane_head = lax.broadcasted_iota(jnp.int32, (1, gp), 1) // hp

    grp = pl.program_id(1)
    row8 = lax.broadcasted_iota(jnp.int32, (SUBLANE, LANE), 0)
    causal = (ri >= ci, ri <= ci)
    first = lane_s < hp
    neg_a = -jnp.exp(alog_ref[...])

    def intra(i, carry):
        rows = pl.ds(pl.multiple_of(i * c, c), c)
        dt_c = _softplus(dt_ref[rows, :] + dtb_ref[...])
        a_c = neg_a * dt_c
        cum = (_cumsum_rows(a_c, False, row8), _cumsum_rows(a_c, True, row8))
        dt_t = dt_c.T
        cum_t = (cum[0].T, cum[1].T)
        xsb = xs_scr[rows, :].astype(BF16)
        bm_t = bs_scr[rows, :].T
        cm = cs_scr[rows, :].astype(BF16)
        g = _dot(cm, bm_t.astype(BF16))
        rhs = jnp.concatenate([jnp.where(lane_head == h, xsb, jnp.zeros_like(xsb)) for h in range(n_hg)], axis=0)
        s_parts, bw_parts = [], ([], [])
        for d in range(2):
            wides = []
            for h in range(n_hg):
                k = d * SSD_H + h
                pick = lambda x: jnp.where(grp == 0, x[k:k + 1], x[k + n_hg:k + n_hg + 1])
                col = jnp.broadcast_to(jnp.where(grp == 0, cum[d][:, k:k + 1], cum[d][:, k + n_hg:k + n_hg + 1]),
                                       (c, LANE))
                rw, dtr = pick(cum_t[d]), pick(dt_t)
                tot = rw[:, c - 1:c] if d == 0 else rw[:, 0:1]
                decay = jnp.exp(jnp.where(causal[d], col - rw, NEG_BIG))
                s_parts.append((g * decay * dtr).astype(BF16))
                bw_parts[d].append((bm_t * (dtr * jnp.exp(tot - rw))).astype(BF16))
                wides.append(col)
            cum_x = jnp.concatenate([jnp.where(first, wides[2 * j], wides[2 * j + 1]) for j in range(gp // LANE)],
                                    axis=1)
            ecx_scr[d, rows, :] = jnp.exp(cum_x)
            dtot_scr[i, d] = jnp.exp(cum_x[c - 1:c] if d == 0 else cum_x[0:1])
        ya_scr[rows, :] = _dot(jnp.concatenate(s_parts, axis=1), jnp.concatenate([rhs, rhs], axis=0))
        inc_scr[i] = _dot(jnp.concatenate([jnp.concatenate(bw_parts[0], axis=1),
                                           jnp.concatenate(bw_parts[1], axis=1)], axis=0), rhs)
        return carry

    lax.fori_loop(0, n_chunks, intra, 0, unroll=2)

    st_scr[0] = h0_ref[0].reshape(gp, SSD_N).T
    st_scr[1] = h0_ref[1].reshape(gp, SSD_N).T

    def inter(i, carry):
        for d in range(2):
            j = i if d == 0 else n_chunks - 1 - i
            rows = pl.ds(pl.multiple_of(j * c, c), c)
            st = st_scr[d]
            ya_scr[rows, :] += _dot(cs_scr[rows, :].astype(BF16), st.astype(BF16)) * ecx_scr[d, rows, :]
            st_scr[d] = st * dtot_scr[j, d] + inc_scr[j, d * SSD_N:(d + 1) * SSD_N, :]
        return carry

    lax.fori_loop(0, n_chunks, inter, 0, unroll=min(4, n_chunks))
    hn_ref[0] = st_scr[0].T.reshape(n_hg, hp, SSD_N)
    hn_ref[1] = st_scr[1].T.reshape(n_hg, hp, SSD_N)

    def finish(i, carry):
        rows = pl.ds(pl.multiple_of(i * c, c), c)
        y = ya_scr[rows, :] + dd_ref[...] * xs_scr[rows, :]
        y = y * _silu(z_ref[rows, :])
        y_ref[rows, :] = _rms(y, ng_ref[...]).astype(BF16)
        return carry

    lax.fori_loop(0, n_chunks, finish, 0, unroll=min(4, n_chunks))


def _ssd(proj3d, cw, cb, dtb, alog, dd, ng, h0, layer, width, gw_blocks):
    b, seq_len, _ = proj3d.shape
    gw = gw_blocks * LANE
    gp = gw // SSD_G
    n_hg = SSD_H // SSD_G
    hp = gp // n_hg
    xw = gp // LANE
    base_z = 9 * gw_blocks
    base_x = 10 * gw_blocks
    base_b = 11 * gw_blocks
    nb = SSD_N // LANE
    base_dt = base_b + 2 * SSD_G * nb
    return pl.pallas_call(
        functools.partial(_ssd_kernel, seq_len=seq_len, width=width),
        grid=(b, SSD_G),
        in_specs=[pl.BlockSpec((None, seq_len, gp), lambda bi, g: (bi, 0, base_z // xw + g)),
                  pl.BlockSpec((None, seq_len, gp), lambda bi, g: (bi, 0, base_x // xw + g)),
                  pl.BlockSpec((None, seq_len, SSD_N), lambda bi, g: (bi, 0, base_b + g)),
                  pl.BlockSpec((None, seq_len, SSD_N), lambda bi, g: (bi, 0, base_b + SSD_G * nb + g)),
                  pl.BlockSpec((None, seq_len, LANE), lambda bi, g: (bi, 0, base_dt)),
                  pl.BlockSpec((None, CONV_K, gp), lambda bi, g: (layer, 0, g)),
                  pl.BlockSpec((None, CONV_K, SSD_N), lambda bi, g: (layer, 0, gw // SSD_N + g)),
                  pl.BlockSpec((None, CONV_K, SSD_N), lambda bi, g: (layer, 0, gw // SSD_N + SSD_G + g)),
                  pl.BlockSpec((None, 1, gp), lambda bi, g: (layer, 0, g)),
                  pl.BlockSpec((None, 1, SSD_N), lambda bi, g: (layer, 0, gw // SSD_N + g)),
                  pl.BlockSpec((None, 1, SSD_N), lambda bi, g: (layer, 0, gw // SSD_N + SSD_G + g)),
                  pl.BlockSpec((None, 1, LANE), lambda bi, g: (layer, 0, 0)),
                  pl.BlockSpec((None, 1, LANE), lambda bi, g: (layer, 0, 0)),
                  pl.BlockSpec((None, 1, gp), lambda bi, g: (layer, 0, g)),
                  pl.BlockSpec((None, 1, gp), lambda bi, g: (layer, 0, g)),
                  pl.BlockSpec((None, 2, n_hg, hp, SSD_N), lambda bi, g: (bi, 0, g, 0, 0))],
        out_specs=[pl.BlockSpec((None, seq_len, gp), lambda bi, g: (bi, 0, g)),
                   pl.BlockSpec((None, 2, n_hg, hp, SSD_N), lambda bi, g: (bi, 0, g, 0, 0))],
        out_shape=[jax.ShapeDtypeStruct((b, seq_len, gw), BF16),
                   jax.ShapeDtypeStruct((b, 2, SSD_H, hp, SSD_N), F32)],
        scratch_shapes=[pltpu.VMEM((seq_len, gp), F32), pltpu.VMEM((seq_len, SSD_N), F32),
                        pltpu.VMEM((seq_len, SSD_N), F32), pltpu.VMEM((seq_len, gp), F32),
                        pltpu.VMEM((2, seq_len, gp), F32),
                        pltpu.VMEM((seq_len // CHUNK, 2 * SSD_N, gp), F32),
                        pltpu.VMEM((seq_len // CHUNK, 2, 1, gp), F32),
                        pltpu.VMEM((2, SSD_N, gp), F32)],
        compiler_params=_cparams("arbitrary", "arbitrary"),
        name="ssd",
    )(proj3d, proj3d, proj3d, proj3d, proj3d, cw, cw, cw, cb, cb, cb,
      dtb, alog, dd, ng, h0)


def _out_proj_kernel(x_ref, ya_ref, yb_ref, yc_ref, yd_ref, gate_ref, w_ref, o_ref):
    gw = ya_ref.shape[-1]
    acc = _dot(ya_ref[...], w_ref[0:gw, :])
    for k, r in enumerate((yb_ref, yc_ref, yd_ref), start=1):
        acc = acc + _dot(r[...], w_ref[k * gw:(k + 1) * gw, :])
    o_ref[...] = x_ref[...] + gate_ref[...] * acc


def _out_proj(x2d, ys, gate, w_out_b, layer, seq_len):
    t, d = x2d.shape
    gw = ys[0].shape[-1]
    shared = gate.shape[0] == 1
    tm = _token_tile(t, seq_len, shared, 512)
    per_seq = seq_len // tm
    mod_map = (lambda i: (0, 0, 0)) if shared else (lambda i: (i // per_seq, 0, 0))
    y_spec = pl.BlockSpec((tm, gw), lambda i: (i, 0))
    return pl.pallas_call(
        _out_proj_kernel,
        grid=(t // tm,),
        in_specs=[pl.BlockSpec((tm, d), lambda i: (i, 0)), y_spec, y_spec, y_spec, y_spec,
                  pl.BlockSpec((None, 1, d), mod_map),
                  pl.BlockSpec((None, N_GROUPS * gw, d), lambda i: (layer, 0, 0))],
        out_specs=pl.BlockSpec((tm, d), lambda i: (i, 0)),
        out_shape=jax.ShapeDtypeStruct((t, d), F32),
        compiler_params=_cparams("arbitrary"),
        name="out_proj",
    )(x2d, *ys, gate, w_out_b)


def _ffn_kernel(x_ref, g_ref, sh_ref, sc_ref, gate_ref, w1_ref, w3_ref, w2_ref, fg_ref, o_ref,
                h_scr, acc_scr, *, final_norm):
    j = pl.program_id(1)

    @pl.when(j == 0)
    def _():
        h = _rms(x_ref[...], g_ref[...]) * (1.0 + sc_ref[...]) + sh_ref[...]
        h_scr[...] = h.astype(BF16)
        acc_scr[...] = jnp.zeros_like(acc_scr)

    h = h_scr[...]
    act = (_silu(_dot(h, w1_ref[...])) * _dot(h, w3_ref[...])).astype(BF16)
    acc_scr[...] += _dot(act, w2_ref[...])

    @pl.when(j == pl.num_programs(1) - 1)
    def _():
        xo = x_ref[...] + gate_ref[...] * acc_scr[...]
        if final_norm:
            xo = _rms(xo, fg_ref[...])
        o_ref[...] = xo


def _ffn(x2d, g, sh, sc, gate, w1_b, w3_b, w2_b, fg, layer, seq_len, final_norm):
    t, d = x2d.shape
    dff = w1_b.shape[-1]
    shared = gate.shape[0] == 1
    tm = _token_tile(t, seq_len, shared, 512)
    tf = 512
    per_seq = seq_len // tm
    mod_map = (lambda i, j: (0, 0, 0)) if shared else (lambda i, j: (i // per_seq, 0, 0))
    return pl.pallas_call(
        functools.partial(_ffn_kernel, final_norm=final_norm),
        grid=(t // tm, dff // tf),
        in_specs=[pl.BlockSpec((tm, d), lambda i, j: (i, 0)),
                  pl.BlockSpec((None, 1, d), lambda i, j: (layer, 0, 0)),
                  pl.BlockSpec((None, 1, d), mod_map),
                  pl.BlockSpec((None, 1, d), mod_map),
                  pl.BlockSpec((None, 1, d), mod_map),
                  pl.BlockSpec((None, d, tf), lambda i, j: (layer, 0, j)),
                  pl.BlockSpec((None, d, tf), lambda i, j: (layer, 0, j)),
                  pl.BlockSpec((None, tf, d), lambda i, j: (layer, j, 0)),
                  pl.BlockSpec((1, d), lambda i, j: (0, 0))],
        out_specs=pl.BlockSpec((tm, d), lambda i, j: (i, 0)),
        out_shape=jax.ShapeDtypeStruct((t, d), F32),
        scratch_shapes=[pltpu.VMEM((tm, d), BF16), pltpu.VMEM((tm, d), F32)],
        compiler_params=_cparams("arbitrary", "arbitrary"),
        name="ffn",
    )(x2d, g, sh, sc, gate, w1_b, w3_b, w2_b, fg)


def _block_diag_pairs(w):
    depth, two, h, hd, _ = w.shape
    w = w.reshape(depth, two, h // 2, 2, hd, hd)
    z = jnp.zeros_like(w[:, :, :, 0])
    top = jnp.concatenate([w[:, :, :, 0], z], axis=-1)
    bot = jnp.concatenate([z, w[:, :, :, 1]], axis=-1)
    return jnp.concatenate([top, bot], axis=-2)


def _head_lanes(v):
    depth = v.shape[0]
    flat = v.reshape(depth, 1, 2 * SSD_H)
    return jnp.pad(flat, ((0, 0), (0, 0), (0, LANE - 2 * SSD_H)))


def kernel(x_prompt, x_sample, state_hgrn, state_rglru, state_ssd, c, c_ctx, w_mod, b_mod, norm1_g, norm2_g, w_in, w_out, gmlp_norm_g, gmlp_ws, gmlp_bs, hgrn_lb, hgrn_norm_g, lru_conv_w, lru_conv_b, lru_wr, lru_br, lru_wi, lru_bi, lru_lambda, ssd_conv_w, ssd_conv_b, ssd_dt_bias, ssd_a_log, ssd_d, ssd_norm_g, ffn_w1, ffn_w3, ffn_w2, final_norm_g):
    depth, d_model, d_in = w_in.shape
    gw = d_model // N_GROUPS
    gwb = gw // LANE
    gp = gw // SSD_G
    assert gw // HG_H == LANE and SSD_N == LANE and gp % LANE == 0
    d_in_pad = -(-d_in // IN_PROJ_TN) * IN_PROJ_TN

    w_in_p = jnp.pad(w_in.astype(BF16), ((0, 0), (0, 0), (0, d_in_pad - d_in)))
    w_out_b = w_out.astype(BF16)
    w1_b, w3_b, w2_b = ffn_w1.astype(BF16), ffn_w3.astype(BF16), ffn_w2.astype(BF16)
    row3 = lambda a: a.reshape(depth, 1, -1)
    ws_b = gmlp_ws.astype(BF16)
    bs_full = jnp.broadcast_to(gmlp_bs[..., None], gmlp_bs.shape + (gw // A_H,))
    lbs = jax.nn.softmax(hgrn_lb.astype(F32), axis=0)
    lb = row3(jnp.cumsum(lbs, axis=0) - lbs[0])
    m_tab = tuple(jnp.asarray(t, F32) for t in _hgrn_masks())
    wr_bd, wi_bd = _block_diag_pairs(lru_wr), _block_diag_pairs(lru_wi)
    wcat = jnp.concatenate([wr_bd[:, 0], wi_bd[:, 0], wr_bd[:, 1], wi_bd[:, 1]], axis=-1).astype(BF16)
    grp = lambda a: a.reshape(depth, 2, gwb, LANE)
    bcat = jnp.concatenate([grp(lru_br)[:, 0], grp(lru_bi)[:, 0], grp(lru_br)[:, 1], grp(lru_bi)[:, 1]],
                           axis=-1).reshape(depth, gwb, 1, 4 * LANE)
    dtb_e, alog_e = _head_lanes(ssd_dt_bias), _head_lanes(ssd_a_log)
    dd_e = row3(jnp.repeat(ssd_d, gw // SSD_H, axis=-1))
    fg = final_norm_g.reshape(1, d_model)

    n_dec = c.shape[0]
    rows = -(-(n_dec + 1) // SUBLANE) * SUBLANE
    cond = jnp.zeros((rows, d_model), F32).at[:n_dec].set(c).at[n_dec].set(c_ctx)
    mod = _modulation(cond, w_mod, b_mod)

    def run_pass(x, row_lo, row_hi, width, st_hg, st_lru, st_ssd):
        b, seq_len, _ = x.shape
        x2d = x.reshape(b * seq_len, d_model)
        s_hg, s_lru, s_ssd = [], [], []
        for l in range(depth):
            m = mod[l, row_lo:row_hi].reshape(row_hi - row_lo, 1, 6, d_model)
            sh1, sc1, g1, sh2, sc2, g2 = (m[:, :, k] for k in range(6))
            proj2d = _in_proj(x2d, row3(norm1_g), sh1, sc1, w_in_p, l, seq_len)
            proj3d = proj2d.reshape(b, seq_len, d_in_pad)
            ya = _gmlp(proj2d, row3(gmlp_norm_g), ws_b, bs_full, l, seq_len)
            yb, hg = _hgrn(proj3d, lb, row3(hgrn_norm_g), st_hg[l], m_tab, l, gwb)
            yc, lr = _lru(proj3d, lru_conv_w, row3(lru_conv_b), wcat, bcat, lru_lambda, st_lru[l], l, width, gwb)
            yd, sd = _ssd(proj3d, ssd_conv_w, row3(ssd_conv_b), dtb_e, alog_e,
                          dd_e, row3(ssd_norm_g), st_ssd[l], l, width, gwb)
            ys = [ya, yb.reshape(-1, gw), yc.reshape(-1, gw), yd.reshape(-1, gw)]
            x2d = _out_proj(x2d, ys, g1, w_out_b, l, seq_len)
            x2d = _ffn(x2d, row3(norm2_g), sh2, sc2, g2, w1_b, w3_b, w2_b, fg, l, seq_len, l == depth - 1)
            s_hg.append(hg)
            s_lru.append(lr)
            s_ssd.append(sd)
        return x2d.reshape(b, seq_len, d_model), s_hg, s_lru, s_ssd

    bp, seq, _ = x_prompt.shape
    hd = gw // HG_H
    z_hg = [jnp.zeros((bp, 2, HG_H, hd, hd), F32)] * depth
    z_lru = [jnp.zeros((bp, 2, gw), F32)] * depth
    z_ssd = [jnp.zeros((bp, 2, SSD_H, gw // SSD_H, SSD_N), F32)] * depth
    y_prompt, s_hg, s_lru, s_ssd = run_pass(x_prompt, n_dec, n_dec + 1, seq, z_hg, z_lru, z_ssd)

    st_hg = [state_hgrn[:, l] for l in range(depth)]
    st_lru = [state_rglru[:, l] for l in range(depth)]
    st_ssd = [state_ssd[:, l] for l in range(depth)]
    y_sample, _, _, _ = run_pass(x_sample, 0, n_dec, GRID_W, st_hg, st_lru, st_ssd)

    return (y_prompt, y_sample, jnp.stack(s_hg, axis=1), jnp.stack(s_lru, axis=1), jnp.stack(s_ssd, axis=1))
```

```python
import functools

import numpy as np
import jax
import jax.numpy as jnp
from jax import lax
from jax.experimental import pallas as pl
from jax.experimental.pallas import tpu as pltpu

F32 = jnp.float32
BF16 = jnp.bfloat16

EPS = 1e-6
N_GROUPS = 4
GRID_W = 64
A_CHUNK = 128
A_H = 4
HG_H = 4
LRU_H = 8
LRU_C = 8.0
CONV_K = 4
CONV_LEFT = 2
SSD_H = 8
SSD_G = 2
SSD_N = 128
LANE = 128
SUBLANE = 8
CHUNK = 128
MXU_N = 256
IN_PROJ_TN = 5 * MXU_N
NORM_ROWS = 256
VMEM_LIMIT = 56 * 1024 * 1024
NEG_BIG = -1e30


def _dot(a, b):
    return jnp.dot(a, b, preferred_element_type=F32)


def _dot_nt(a, b):
    return lax.dot_general(a, b, (((1,), (1,)), ((), ())), preferred_element_type=F32)


def _dot_tn(a, b):
    return lax.dot_general(a, b, (((0,), (0,)), ((), ())), preferred_element_type=F32)


def _split_terms(x, terms):
    hi = x.astype(BF16).astype(F32)
    r = x - hi
    if terms == 2:
        return [hi, r]
    mid = r.astype(BF16).astype(F32)
    return [hi, mid, r - mid]


def _dot01(m01, x, terms=3):
    return sum(_dot(m01, t.astype(BF16)) for t in _split_terms(x, terms))


def _sigmoid(x):
    return 0.5 * jnp.tanh(0.5 * x) + 0.5


def _sigmoid_rel(x):
    return jnp.exp(jnp.minimum(x, 0.0) - jnp.log(1.0 + jnp.exp(-jnp.abs(x))))


def _silu(x):
    return x * _sigmoid(x)


def _softplus(x):
    return jnp.maximum(x, 0.0) + jnp.log1p(jnp.exp(-jnp.abs(x)))


def _log_sigmoid(x):
    return jnp.minimum(x, 0.0) - jnp.log1p(jnp.exp(-jnp.abs(x)))


def _rms(x, g):
    return x * lax.rsqrt(jnp.mean(x * x, axis=-1, keepdims=True) + EPS) * g


def _conv_block(x, w, b, pos, width):
    n = x.shape[0]
    y = b + w[CONV_LEFT:CONV_LEFT + 1] * x
    for j in range(CONV_K):
        s = j - CONV_LEFT
        if s == 0:
            continue
        xs = pltpu.roll(x, (-s) % n, 0)
        ok = (pos + s >= 0) & (pos + s < width)
        y = y + w[j:j + 1] * jnp.where(ok, xs, 0.0)
    return y


def _cparams(*sem):
    return pltpu.CompilerParams(dimension_semantics=sem, vmem_limit_bytes=VMEM_LIMIT)


def _token_tile(t, seq_len, shared, cap):
    tm = min(cap, t if shared else seq_len)
    assert t % tm == 0 and (shared or seq_len % tm == 0)
    return tm


def _mod_kernel(c_ref, w_ref, b_ref, o_ref):
    s = _silu(c_ref[...]).astype(BF16)
    o_ref[...] = _dot(s, w_ref[...].astype(BF16)) + b_ref[...]


def _modulation(cond, w_mod, b_mod):
    depth, d, n = w_mod.shape
    rows = cond.shape[0]
    tn = 1024
    return pl.pallas_call(
        _mod_kernel,
        grid=(depth, n // tn),
        in_specs=[pl.BlockSpec((rows, d), lambda l, j: (0, 0)),
                  pl.BlockSpec((None, d, tn), lambda l, j: (l, 0, j)),
                  pl.BlockSpec((None, 1, tn), lambda l, j: (l, 0, j))],
        out_specs=pl.BlockSpec((None, rows, tn), lambda l, j: (l, 0, j)),
        out_shape=jax.ShapeDtypeStruct((depth, rows, n), F32),
        compiler_params=_cparams("arbitrary", "arbitrary"),
        name="modulation",
    )(cond, w_mod, b_mod.reshape(depth, 1, n))


def _norm_blocks(tm):
    rb = min(NORM_ROWS, tm)
    return [slice(r, r + rb) for r in range(0, tm, rb)]


def _in_proj_kernel(x_ref, g_ref, sh_ref, sc_ref, w_ref, o_ref, h_scr):
    j = pl.program_id(1)

    @pl.when(j == 0)
    def _():
        for rows in _norm_blocks(x_ref.shape[0]):
            h = (_rms(x_ref[rows, :], g_ref[...]) * (1.0 + sc_ref[...]) + sh_ref[...]).astype(BF16)
            h_scr[rows, :] = h
            o_ref[rows, :] = _dot(h, w_ref[...])

    @pl.when(j != 0)
    def _():
        o_ref[...] = _dot(h_scr[...], w_ref[...])


def _in_proj(x2d, g, sh, sc, w_in_p, layer, seq_len):
    t, d = x2d.shape
    n = w_in_p.shape[-1]
    shared = sh.shape[0] == 1
    tm = _token_tile(t, seq_len, shared, 1024)
    tn = IN_PROJ_TN
    per_seq = seq_len // tm
    mod_map = (lambda i, j: (0, 0, 0)) if shared else (lambda i, j: (i // per_seq, 0, 0))
    return pl.pallas_call(
        _in_proj_kernel,
        grid=(t // tm, n // tn),
        in_specs=[pl.BlockSpec((tm, d), lambda i, j: (i, 0)),
                  pl.BlockSpec((None, 1, d), lambda i, j: (layer, 0, 0)),
                  pl.BlockSpec((None, 1, d), mod_map),
                  pl.BlockSpec((None, 1, d), mod_map),
                  pl.BlockSpec((None, d, tn), lambda i, j: (layer, 0, j))],
        out_specs=pl.BlockSpec((tm, tn), lambda i, j: (i, j)),
        out_shape=jax.ShapeDtypeStruct((t, n), F32),
        scratch_shapes=[pltpu.VMEM((tm, d), BF16)],
        compiler_params=_cparams("arbitrary", "arbitrary"),
        name="in_proj",
    )(x2d, g, sh, sc, w_in_p)


def _gmlp_kernel(u_ref, v_ref, g_ref, ws_ref, bs_ref, o_ref):
    v = _rms(jax.nn.gelu(v_ref[...]), g_ref[...]).astype(BF16)
    u = jax.nn.gelu(u_ref[...])
    tm, gw = u.shape
    hd = gw // A_H
    for c in range(tm // A_CHUNK):
        rows = slice(c * A_CHUNK, (c + 1) * A_CHUNK)
        for h in range(A_H):
            cols = slice(h * hd, (h + 1) * hd)
            r = _dot(ws_ref[h], v[rows, cols]) + bs_ref[h]
            o_ref[rows, cols] = (u[rows, cols] * r).astype(BF16)


def _gmlp(proj2d, g, ws_b, bs_full, layer, seq_len):
    t = proj2d.shape[0]
    gw = g.shape[-1]
    tm = _token_tile(t, seq_len, True, 512)
    return pl.pallas_call(
        _gmlp_kernel,
        grid=(t // tm,),
        in_specs=[pl.BlockSpec((tm, gw), lambda i: (i, 0)),
                  pl.BlockSpec((tm, gw), lambda i: (i, 1)),
                  pl.BlockSpec((None, 1, gw), lambda i: (layer, 0, 0)),
                  pl.BlockSpec((None, A_H, A_CHUNK, A_CHUNK), lambda i: (layer, 0, 0, 0)),
                  pl.BlockSpec((None, A_H, A_CHUNK, gw // A_H), lambda i: (layer, 0, 0, 0))],
        out_specs=pl.BlockSpec((tm, gw), lambda i: (i, 0)),
        out_shape=jax.ShapeDtypeStruct((t, gw), BF16),
        compiler_params=_cparams("arbitrary"),
        name="gmlp",
    )(proj2d, proj2d, g, ws_b, bs_full)


HG_LEVELS = int(np.log2(CHUNK))


def _hgrn_masks():
    c = CHUNK
    mk = np.zeros((HG_LEVELS, c, c), np.float32)
    sg = np.zeros((HG_LEVELS, c, LANE), np.float32)
    u = np.arange(c)
    for lv in range(HG_LEVELS):
        blk = u // (c >> lv)
        later = (u % (c >> lv)) >= (c >> (lv + 1))
        mk[lv] = blk[:, None] == blk[None, :]
        sg[lv] = np.where(later, np.log2(np.e), -np.log2(np.e))[:, None]
    return mk, sg


def _cumsum_rows(x, reverse, row8):
    n = x.shape[0] // SUBLANE
    tiles = []
    for i in range(n):
        t = x[i * SUBLANE:(i + 1) * SUBLANE]
        d = 1
        while d < SUBLANE:
            shift = SUBLANE - d if reverse else d
            ok = (row8 < SUBLANE - d) if reverse else (row8 >= d)
            t = t + jnp.where(ok, pltpu.roll(t, shift, 0), 0.0)
            d *= 2
        tiles.append(t)
    carry = None
    for i in (range(n - 1, -1, -1) if reverse else range(n)):
        if carry is not None:
            tiles[i] = tiles[i] + carry
        carry = tiles[i][0:1] if reverse else tiles[i][SUBLANE - 1:SUBLANE]
    return jnp.concatenate(tiles, axis=0)


def _level_ref(cum, m, d, rowc, row8):
    c, hd = cum.shape
    off = m - 1 if d == 0 else m
    if m == 1:
        odd = (rowc % 2) == 1
        if d == 0:
            return jnp.where(odd, pltpu.roll(cum, 1, 0), cum)
        return jnp.where(odd, cum, pltpu.roll(cum, c - 1, 0))
    bcast = lambda r, n: jnp.broadcast_to(cum[r:r + 1, :], (n, hd))
    if 2 * m < SUBLANE:
        per = SUBLANE // (2 * m)
        pieces = []
        for v in range(c // SUBLANE):
            piece = bcast(v * SUBLANE + off, SUBLANE)
            for k in range(1, per):
                piece = jnp.where(row8 >= k * 2 * m, bcast(v * SUBLANE + k * 2 * m + off, SUBLANE), piece)
            pieces.append(piece)
        return jnp.concatenate(pieces, axis=0)
    return jnp.concatenate([bcast(k * 2 * m + off, 2 * m) for k in range(c // (2 * m))], axis=0)


def _hgrn_kernel(q_ref, ff_ref, fb_ref, i_ref, gt_ref, lb_ref, ng_ref, s0_ref, m_ref, sg_ref,
                 y_ref, sn_ref, o_scr, qd_scr, kv_scr, dec_scr, st_scr, *, seq_len):
    c = CHUNK
    n_chunks = seq_len // c
    hd = q_ref.shape[-1]
    lb = lb_ref[...]
    rowc = lax.broadcasted_iota(jnp.int32, (c, hd), 0)
    row8 = lax.broadcasted_iota(jnp.int32, (SUBLANE, hd), 0)
    f_refs = (ff_ref, fb_ref)

    def intra(i, carry):
        rows = pl.ds(pl.multiple_of(i * c, c), c)
        q = _silu(q_ref[rows, :])
        v = i_ref[rows, :]
        vb = v.astype(BF16)
        kk, cum, tot = [], [], []
        for d in range(2):
            f = lb + (1.0 - lb) * _sigmoid_rel(f_refs[d][rows, :])
            kk.append(1.0 - f)
            cum.append(_cumsum_rows(jnp.log(f), d == 1, row8))
            tot.append(cum[d][c - 1:c] if d == 0 else cum[d][0:1])
        scores = jnp.zeros((c, c), F32)
        for lv in range(HG_LEVELS):
            m = c >> (lv + 1)
            later = (rowc % (2 * m)) >= m
            sg = sg_ref[lv]
            el = [jnp.exp2((cum[0] - _level_ref(cum[0], m, 0, rowc, row8)) * sg),
                  jnp.exp2((_level_ref(cum[1], m, 1, rowc, row8) - cum[1]) * sg)]
            ql = jnp.concatenate([jnp.where(later, q * el[0], 0.0).astype(BF16),
                                  jnp.where(later, 0.0, q * el[1]).astype(BF16)], axis=1)
            kl = jnp.concatenate([jnp.where(later, 0.0, kk[0] * el[0]).astype(BF16),
                                  jnp.where(later, kk[1] * el[1], 0.0).astype(BF16)], axis=1)
            scores = scores + _dot_nt(ql, kl) * m_ref[lv]
        o_scr[rows, :] = (_dot(scores.astype(BF16), vb)
                          + jnp.sum(q * (kk[0] + kk[1]), axis=-1, keepdims=True) * v)
        kd = [(kk[d] * jnp.exp(tot[d] - cum[d])).astype(BF16) for d in range(2)]
        kv_scr[i] = _dot_tn(vb, jnp.concatenate(kd, axis=1))
        for d in range(2):
            qd_scr[d, rows, :] = (q * jnp.exp(cum[d])).astype(BF16)
            dec_scr[i, :, d * hd:(d + 1) * hd] = jnp.exp(tot[d])
        return carry

    lax.fori_loop(0, n_chunks, intra, 0, unroll=2)

    st_scr[0] = s0_ref[0].T
    st_scr[1] = s0_ref[1].T

    def inter(i, carry):
        for d in range(2):
            j = i if d == 0 else n_chunks - 1 - i
            rows = pl.ds(pl.multiple_of(j * c, c), c)
            st = st_scr[d]
            o_scr[rows, :] += _dot_nt(qd_scr[d, rows, :], st.astype(BF16))
            st_scr[d] = st * dec_scr[j, :, d * hd:(d + 1) * hd] + kv_scr[j, :, d * hd:(d + 1) * hd]
        return carry

    lax.fori_loop(0, n_chunks, inter, 0, unroll=min(4, n_chunks))
    sn_ref[0] = st_scr[0].T
    sn_ref[1] = st_scr[1].T

    def finish(i, carry):
        rows = pl.ds(pl.multiple_of(i * c, c), c)
        y_ref[rows, :] = (_rms(o_scr[rows, :], ng_ref[...]) * _silu(gt_ref[rows, :])).astype(BF16)
        return carry

    lax.fori_loop(0, n_chunks, finish, 0, unroll=min(4, n_chunks))


def _hgrn(proj3d, lb, ng, s0, m_tab, layer, gw_blocks):
    b, seq_len, _ = proj3d.shape
    hd = LANE
    n_chunks = seq_len // CHUNK
    base = 2 * gw_blocks
    col = lambda k: (lambda bi, h: (bi, 0, base + k * gw_blocks + h))
    specs = [pl.BlockSpec((None, seq_len, hd), col(k)) for k in range(5)]
    return pl.pallas_call(
        functools.partial(_hgrn_kernel, seq_len=seq_len),
        grid=(b, HG_H),
        in_specs=specs + [
            pl.BlockSpec((None, 1, hd), lambda bi, h: (layer, 0, h)),
            pl.BlockSpec((None, 1, hd), lambda bi, h: (layer, 0, 0)),
            pl.BlockSpec((None, 2, None, hd, hd), lambda bi, h: (bi, 0, h, 0, 0)),
            pl.BlockSpec((HG_LEVELS, CHUNK, CHUNK), lambda bi, h: (0, 0, 0)),
            pl.BlockSpec((HG_LEVELS, CHUNK, hd), lambda bi, h: (0, 0, 0))],
        out_specs=[pl.BlockSpec((None, seq_len, hd), lambda bi, h: (bi, 0, h)),
                   pl.BlockSpec((None, 2, None, hd, hd), lambda bi, h: (bi, 0, h, 0, 0))],
        out_shape=[jax.ShapeDtypeStruct((b, seq_len, HG_H * hd), BF16),
                   jax.ShapeDtypeStruct((b, 2, HG_H, hd, hd), F32)],
        scratch_shapes=[pltpu.VMEM((seq_len, hd), F32), pltpu.VMEM((2, seq_len, hd), BF16),
                        pltpu.VMEM((n_chunks, hd, 2 * hd), F32), pltpu.VMEM((n_chunks, 1, 2 * hd), F32),
                        pltpu.VMEM((2, hd, hd), F32)],
        compiler_params=_cparams("arbitrary", "arbitrary"),
        name="hgrn",
    )(proj3d, proj3d, proj3d, proj3d, proj3d, lb, ng, s0, *m_tab)


SCAN_ROWS = 64


def _scan_block(a_blk, b_blk, carry, row, reverse):
    n_tiles = a_blk.shape[0] // SUBLANE
    outs = [None] * n_tiles
    order = range(n_tiles - 1, -1, -1) if reverse else range(n_tiles)
    for i in order:
        a = a_blk[i * SUBLANE:(i + 1) * SUBLANE]
        bb = b_blk[i * SUBLANE:(i + 1) * SUBLANE]
        d = 1
        while d < SUBLANE:
            shift = SUBLANE - d if reverse else d
            ok = (row < SUBLANE - d) if reverse else (row >= d)
            a_s = jnp.where(ok, pltpu.roll(a, shift, 0), 1.0)
            b_s = jnp.where(ok, pltpu.roll(bb, shift, 0), 0.0)
            bb = a * b_s + bb
            a = a * a_s
            d *= 2
        h = bb + a * carry
        carry = h[0:1] if reverse else h[SUBLANE - 1:SUBLANE]
        outs[i] = h
    return jnp.concatenate(outs, axis=0), carry


def _prep_rows(seq_len, width):
    rows = max(width, min(256, seq_len))
    assert rows % width == 0 and seq_len % rows == 0
    return rows


def _lru_kernel(cx_ref, cg_ref, cw_ref, cb_ref, w_ref, bias_ref, lam_ref, h0_ref,
                y_ref, hn_ref, a_scr, b_scr, h_scr, *, seq_len, width):
    lanes = cx_ref.shape[-1]
    pr = _prep_rows(seq_len, width)
    pos = lax.broadcasted_iota(jnp.int32, (pr, lanes), 0) % width
    lsl = LRU_C * _log_sigmoid(lam_ref[...])

    def prep(i, carry):
        rows = pl.ds(pl.multiple_of(i * pr, pr), pr)
        xc = _conv_block(cx_ref[rows, :], cw_ref[...], cb_ref[...], pos, width)
        gates = _sigmoid(_dot(xc.astype(BF16), w_ref[...]) + bias_ref[...])
        for d in range(2):
            r = gates[:, (2 * d) * lanes:(2 * d + 1) * lanes]
            ig = gates[:, (2 * d + 1) * lanes:(2 * d + 2) * lanes]
            log_a = r * lsl[d:d + 1]
            a = jnp.exp(log_a)
            w = -jnp.tanh(log_a) * (a * a + 1.0)
            root = jnp.where(w > 0.0, w * lax.rsqrt(w), 0.0)
            bx = root * (ig * xc)
            a_scr[d, rows, :] = a
            b_scr[d, rows, :] = bx
        return carry

    lax.fori_loop(0, seq_len // pr, prep, 0)

    row = lax.broadcasted_iota(jnp.int32, (SUBLANE, lanes), 0)
    n_blocks = seq_len // SCAN_ROWS

    def scan(i, carry):
        cf, cb = carry
        rf = pl.ds(pl.multiple_of(i * SCAN_ROWS, SCAN_ROWS), SCAN_ROWS)
        rb = pl.ds(pl.multiple_of((n_blocks - 1 - i) * SCAN_ROWS, SCAN_ROWS), SCAN_ROWS)
        hf, cf = _scan_block(a_scr[0, rf, :], b_scr[0, rf, :], cf, row, False)
        hb, cb = _scan_block(a_scr[1, rb, :], b_scr[1, rb, :], cb, row, True)
        h_scr[0, rf, :] = hf
        h_scr[1, rb, :] = hb
        return cf, cb

    cf, cb = lax.fori_loop(0, n_blocks, scan, (h0_ref[0:1, :], h0_ref[1:2, :]))
    hn_ref[0:1, :] = cf
    hn_ref[1:2, :] = cb

    def finish(i, carry):
        rows = pl.ds(pl.multiple_of(i * pr, pr), pr)
        y_ref[rows, :] = ((h_scr[0, rows, :] + h_scr[1, rows, :]) * jax.nn.gelu(cg_ref[rows, :])).astype(BF16)
        return carry

    lax.fori_loop(0, seq_len // pr, finish, 0)


def _lru(proj3d, cw, cb, wcat, bcat, lam, h0, layer, width, gw_blocks):
    b, seq_len, _ = proj3d.shape
    n_grp = gw_blocks
    base = 7 * gw_blocks
    return pl.pallas_call(
        functools.partial(_lru_kernel, seq_len=seq_len, width=width),
        grid=(b, n_grp),
        in_specs=[pl.BlockSpec((None, seq_len, LANE), lambda bi, j: (bi, 0, base + j)),
                  pl.BlockSpec((None, seq_len, LANE), lambda bi, j: (bi, 0, base + gw_blocks + j)),
                  pl.BlockSpec((None, CONV_K, LANE), lambda bi, j: (layer, 0, j)),
                  pl.BlockSpec((None, 1, LANE), lambda bi, j: (layer, 0, j)),
                  pl.BlockSpec((None, None, LANE, 4 * LANE), lambda bi, j: (layer, j, 0, 0)),
                  pl.BlockSpec((None, None, 1, 4 * LANE), lambda bi, j: (layer, j, 0, 0)),
                  pl.BlockSpec((None, 2, LANE), lambda bi, j: (layer, 0, j)),
                  pl.BlockSpec((None, 2, LANE), lambda bi, j: (bi, 0, j))],
        out_specs=[pl.BlockSpec((None, seq_len, LANE), lambda bi, j: (bi, 0, j)),
                   pl.BlockSpec((None, 2, LANE), lambda bi, j: (bi, 0, j))],
        out_shape=[jax.ShapeDtypeStruct((b, seq_len, n_grp * LANE), BF16),
                   jax.ShapeDtypeStruct((b, 2, n_grp * LANE), F32)],
        scratch_shapes=[pltpu.VMEM((2, seq_len, LANE), F32), pltpu.VMEM((2, seq_len, LANE), F32),
                        pltpu.VMEM((2, seq_len, LANE), F32)],
        compiler_params=_cparams("arbitrary", "arbitrary"),
        name="rglru",
    )(proj3d, proj3d, cw, cb, wcat, bcat, lam, h0)


def _ssd_kernel(z_ref, x_ref, bm_ref, cm_ref, dt_ref,
                cwx_ref, cwb_ref, cwc_ref, cbx_ref, cbb_ref, cbc_ref,
                dtb_ref, alog_ref, dd_ref, ng_ref, h0_ref,
                y_ref, hn_ref, xs_scr, bs_scr, cs_scr, ya_scr, ecx_scr, inc_scr, dtot_scr, st_scr,
                *, seq_len, width):
    c = CHUNK
    n_chunks = seq_len // c
    gp = x_ref.shape[-1]
    n_hg = SSD_H // SSD_G
    hp = gp // n_hg
    pr = _prep_rows(seq_len, width)
    pos_x = lax.broadcasted_iota(jnp.int32, (pr, gp), 0) % width
    pos_n = lax.broadcasted_iota(jnp.int32, (pr, SSD_N), 0) % width

    def prep(i, carry):
        rows = pl.ds(pl.multiple_of(i * pr, pr), pr)
        xs_scr[rows, :] = _silu(_conv_block(x_ref[rows, :], cwx_ref[...], cbx_ref[...], pos_x, width))
        bs_scr[rows, :] = _silu(_conv_block(bm_ref[rows, :], cwb_ref[...], cbb_ref[...], pos_n, width))
        cs_scr[rows, :] = _silu(_conv_block(cm_ref[rows, :], cwc_ref[...], cbc_ref[...], pos_n, width))
        return carry

    lax.fori_loop(0, seq_len // pr, prep, 0)

    ri = lax.broadcasted_iota(jnp.int32, (c, c), 0)
    ci = lax.broadcasted_iota(jnp.int32, (c, c), 1)
    assert LANE == 2 * hp and SSD_G == 2
    lane_s = lax.broadcasted_iota(jnp.int32, (1, LANE), 1)
    lane_head = lax.broadcasted_iota(jnp.int32, (1, gp), 1) // hp

    grp = pl.program_id(1)
    row8 = lax.broadcasted_iota(jnp.int32, (SUBLANE, LANE), 0)
    causal = (ri >= ci, ri <= ci)
    first = lane_s < hp
    neg_a = -jnp.exp(alog_ref[...])

    def intra(i, carry):
        rows = pl.ds(pl.multiple_of(i * c, c), c)
        dt_c = _softplus(dt_ref[rows, :] + dtb_ref[...])
        a_c = neg_a * dt_c
        cum = (_cumsum_rows(a_c, False, row8), _cumsum_rows(a_c, True, row8))
        dt_t = dt_c.T
        cum_t = (cum[0].T, cum[1].T)
        xsb = xs_scr[rows, :].astype(BF16)
        bm_t = bs_scr[rows, :].T
        cm = cs_scr[rows, :].astype(BF16)
        g = _dot(cm, bm_t.astype(BF16))
        rhs = jnp.concatenate([jnp.where(lane_head == h, xsb, jnp.zeros_like(xsb)) for h in range(n_hg)], axis=0)
        s_parts, bw_parts = [], ([], [])
        for d in range(2):
            wides = []
            for h in range(n_hg):
                k = d * SSD_H + h
                pick = lambda x: jnp.where(grp == 0, x[k:k + 1], x[k + n_hg:k + n_hg + 1])
                col = jnp.broadcast_to(jnp.where(grp == 0, cum[d][:, k:k + 1], cum[d][:, k + n_hg:k + n_hg + 1]),
                                       (c, LANE))
                rw, dtr = pick(cum_t[d]), pick(dt_t)
                tot = rw[:, c - 1:c] if d == 0 else rw[:, 0:1]
                decay = jnp.exp(jnp.where(causal[d], col - rw, NEG_BIG))
                s_parts.append((g * decay * dtr).astype(BF16))
                bw_parts[d].append((bm_t * (dtr * jnp.exp(tot - rw))).astype(BF16))
                wides.append(col)
            cum_x = jnp.concatenate([jnp.where(first, wides[2 * j], wides[2 * j + 1]) for j in range(gp // LANE)],
                                    axis=1)
            ecx_scr[d, rows, :] = jnp.exp(cum_x)
            dtot_scr[i, d] = jnp.exp(cum_x[c - 1:c] if d == 0 else cum_x[0:1])
        ya_scr[rows, :] = _dot(jnp.concatenate(s_parts, axis=1), jnp.concatenate([rhs, rhs], axis=0))
        inc_scr[i] = _dot(jnp.concatenate([jnp.concatenate(bw_parts[0], axis=1),
                                           jnp.concatenate(bw_parts[1], axis=1)], axis=0), rhs)
        return carry

    lax.fori_loop(0, n_chunks, intra, 0, unroll=2)

    st_scr[0] = h0_ref[0].reshape(gp, SSD_N).T
    st_scr[1] = h0_ref[1].reshape(gp, SSD_N).T

    def inter(i, carry):
        for d in range(2):
            j = i if d == 0 else n_chunks - 1 - i
            rows = pl.ds(pl.multiple_of(j * c, c), c)
            st = st_scr[d]
            ya_scr[rows, :] += _dot(cs_scr[rows, :].astype(BF16), st.astype(BF16)) * ecx_scr[d, rows, :]
            st_scr[d] = st * dtot_scr[j, d] + inc_scr[j, d * SSD_N:(d + 1) * SSD_N, :]
        return carry

    lax.fori_loop(0, n_chunks, inter, 0, unroll=min(4, n_chunks))
    hn_ref[0] = st_scr[0].T.reshape(n_hg, hp, SSD_N)
    hn_ref[1] = st_scr[1].T.reshape(n_hg, hp, SSD_N)

    def finish(i, carry):
        rows = pl.ds(pl.multiple_of(i * c, c), c)
        y = ya_scr[rows, :] + dd_ref[...] * xs_scr[rows, :]
        y = y * _silu(z_ref[rows, :])
        y_ref[rows, :] = _rms(y, ng_ref[...]).astype(BF16)
        return carry

    lax.fori_loop(0, n_chunks, finish, 0, unroll=min(4, n_chunks))


def _ssd(proj3d, cw, cb, dtb, alog, dd, ng, h0, layer, width, gw_blocks):
    b, seq_len, _ = proj3d.shape
    gw = gw_blocks * LANE
    gp = gw // SSD_G
    n_hg = SSD_H // SSD_G
    hp = gp // n_hg
    xw = gp // LANE
    base_z = 9 * gw_blocks
    base_x = 10 * gw_blocks
    base_b = 11 * gw_blocks
    nb = SSD_N // LANE
    base_dt = base_b + 2 * SSD_G * nb
    return pl.pallas_call(
        functools.partial(_ssd_kernel, seq_len=seq_len, width=width),
        grid=(b, SSD_G),
        in_specs=[pl.BlockSpec((None, seq_len, gp), lambda bi, g: (bi, 0, base_z // xw + g)),
                  pl.BlockSpec((None, seq_len, gp), lambda bi, g: (bi, 0, base_x // xw + g)),
                  pl.BlockSpec((None, seq_len, SSD_N), lambda bi, g: (bi, 0, base_b + g)),
                  pl.BlockSpec((None, seq_len, SSD_N), lambda bi, g: (bi, 0, base_b + SSD_G * nb + g)),
                  pl.BlockSpec((None, seq_len, LANE), lambda bi, g: (bi, 0, base_dt)),
                  pl.BlockSpec((None, CONV_K, gp), lambda bi, g: (layer, 0, g)),
                  pl.BlockSpec((None, CONV_K, SSD_N), lambda bi, g: (layer, 0, gw // SSD_N + g)),
                  pl.BlockSpec((None, CONV_K, SSD_N), lambda bi, g: (layer, 0, gw // SSD_N + SSD_G + g)),
                  pl.BlockSpec((None, 1, gp), lambda bi, g: (layer, 0, g)),
                  pl.BlockSpec((None, 1, SSD_N), lambda bi, g: (layer, 0, gw // SSD_N + g)),
                  pl.BlockSpec((None, 1, SSD_N), lambda bi, g: (layer, 0, gw // SSD_N + SSD_G + g)),
                  pl.BlockSpec((None, 1, LANE), lambda bi, g: (layer, 0, 0)),
                  pl.BlockSpec((None, 1, LANE), lambda bi, g: (layer, 0, 0)),
                  pl.BlockSpec((None, 1, gp), lambda bi, g: (layer, 0, g)),
                  pl.BlockSpec((None, 1, gp), lambda bi, g: (layer, 0, g)),
                  pl.BlockSpec((None, 2, n_hg, hp, SSD_N), lambda bi, g: (bi, 0, g, 0, 0))],
        out_specs=[pl.BlockSpec((None, seq_len, gp), lambda bi, g: (bi, 0, g)),
                   pl.BlockSpec((None, 2, n_hg, hp, SSD_N), lambda bi, g: (bi, 0, g, 0, 0))],
        out_shape=[jax.ShapeDtypeStruct((b, seq_len, gw), BF16),
                   jax.ShapeDtypeStruct((b, 2, SSD_H, hp, SSD_N), F32)],
        scratch_shapes=[pltpu.VMEM((seq_len, gp), F32), pltpu.VMEM((seq_len, SSD_N), F32),
                        pltpu.VMEM((seq_len, SSD_N), F32), pltpu.VMEM((seq_len, gp), F32),
                        pltpu.VMEM((2, seq_len, gp), F32),
                        pltpu.VMEM((seq_len // CHUNK, 2 * SSD_N, gp), F32),
                        pltpu.VMEM((seq_len // CHUNK, 2, 1, gp), F32),
                        pltpu.VMEM((2, SSD_N, gp), F32)],
        compiler_params=_cparams("arbitrary", "arbitrary"),
        name="ssd",
    )(proj3d, proj3d, proj3d, proj3d, proj3d, cw, cw, cw, cb, cb, cb,
      dtb, alog, dd, ng, h0)


def _out_proj_kernel(x_ref, ya_ref, yb_ref, yc_ref, yd_ref, gate_ref, w_ref, o_ref):
    gw = ya_ref.shape[-1]
    acc = _dot(ya_ref[...], w_ref[0:gw, :])
    for k, r in enumerate((yb_ref, yc_ref, yd_ref), start=1):
        acc = acc + _dot(r[...], w_ref[k * gw:(k + 1) * gw, :])
    o_ref[...] = x_ref[...] + gate_ref[...] * acc


def _out_proj(x2d, ys, gate, w_out_b, layer, seq_len):
    t, d = x2d.shape
    gw = ys[0].shape[-1]
    shared = gate.shape[0] == 1
    tm = _token_tile(t, seq_len, shared, 512)
    per_seq = seq_len // tm
    mod_map = (lambda i: (0, 0, 0)) if shared else (lambda i: (i // per_seq, 0, 0))
    y_spec = pl.BlockSpec((tm, gw), lambda i: (i, 0))
    return pl.pallas_call(
        _out_proj_kernel,
        grid=(t // tm,),
        in_specs=[pl.BlockSpec((tm, d), lambda i: (i, 0)), y_spec, y_spec, y_spec, y_spec,
                  pl.BlockSpec((None, 1, d), mod_map),
                  pl.BlockSpec((None, N_GROUPS * gw, d), lambda i: (layer, 0, 0))],
        out_specs=pl.BlockSpec((tm, d), lambda i: (i, 0)),
        out_shape=jax.ShapeDtypeStruct((t, d), F32),
        compiler_params=_cparams("arbitrary"),
        name="out_proj",
    )(x2d, *ys, gate, w_out_b)


def _ffn_kernel(x_ref, g_ref, sh_ref, sc_ref, gate_ref, w1_ref, w3_ref, w2_ref, fg_ref, o_ref,
                h_scr, acc_scr, *, final_norm):
    j = pl.program_id(1)

    def partial_out(h):
        act = (_silu(_dot(h, w1_ref[...])) * _dot(h, w3_ref[...])).astype(BF16)
        return _dot(act, w2_ref[...])

    @pl.when(j == 0)
    def _():
        for rows in _norm_blocks(x_ref.shape[0]):
            h = (_rms(x_ref[rows, :], g_ref[...]) * (1.0 + sc_ref[...]) + sh_ref[...]).astype(BF16)
            h_scr[rows, :] = h
            acc_scr[rows, :] = partial_out(h)

    @pl.when(j != 0)
    def _():
        acc_scr[...] += partial_out(h_scr[...])

    @pl.when(j == pl.num_programs(1) - 1)
    def _():
        xo = x_ref[...] + gate_ref[...] * acc_scr[...]
        if final_norm:
            xo = _rms(xo, fg_ref[...])
        o_ref[...] = xo


def _ffn(x2d, g, sh, sc, gate, w1_b, w3_b, w2_b, fg, layer, seq_len, final_norm):
    t, d = x2d.shape
    dff = w1_b.shape[-1]
    shared = gate.shape[0] == 1
    tm = _token_tile(t, seq_len, shared, 512)
    tf = 512
    per_seq = seq_len // tm
    mod_map = (lambda i, j: (0, 0, 0)) if shared else (lambda i, j: (i // per_seq, 0, 0))
    return pl.pallas_call(
        functools.partial(_ffn_kernel, final_norm=final_norm),
        grid=(t // tm, dff // tf),
        in_specs=[pl.BlockSpec((tm, d), lambda i, j: (i, 0)),
                  pl.BlockSpec((None, 1, d), lambda i, j: (layer, 0, 0)),
                  pl.BlockSpec((None, 1, d), mod_map),
                  pl.BlockSpec((None, 1, d), mod_map),
                  pl.BlockSpec((None, 1, d), mod_map),
                  pl.BlockSpec((None, d, tf), lambda i, j: (layer, 0, j)),
                  pl.BlockSpec((None, d, tf), lambda i, j: (layer, 0, j)),
                  pl.BlockSpec((None, tf, d), lambda i, j: (layer, j, 0)),
                  pl.BlockSpec((1, d), lambda i, j: (0, 0))],
        out_specs=pl.BlockSpec((tm, d), lambda i, j: (i, 0)),
        out_shape=jax.ShapeDtypeStruct((t, d), F32),
        scratch_shapes=[pltpu.VMEM((tm, d), BF16), pltpu.VMEM((tm, d), F32)],
        compiler_params=_cparams("arbitrary", "arbitrary"),
        name="ffn",
    )(x2d, g, sh, sc, gate, w1_b, w3_b, w2_b, fg)


def _block_diag_pairs(w):
    depth, two, h, hd, _ = w.shape
    w = w.reshape(depth, two, h // 2, 2, hd, hd)
    z = jnp.zeros_like(w[:, :, :, 0])
    top = jnp.concatenate([w[:, :, :, 0], z], axis=-1)
    bot = jnp.concatenate([z, w[:, :, :, 1]], axis=-1)
    return jnp.concatenate([top, bot], axis=-2)


def _head_lanes(v):
    depth = v.shape[0]
    flat = v.reshape(depth, 1, 2 * SSD_H)
    return jnp.pad(flat, ((0, 0), (0, 0), (0, LANE - 2 * SSD_H)))


def kernel(x_prompt, x_sample, state_hgrn, state_rglru, state_ssd, c, c_ctx, w_mod, b_mod, norm1_g, norm2_g, w_in, w_out, gmlp_norm_g, gmlp_ws, gmlp_bs, hgrn_lb, hgrn_norm_g, lru_conv_w, lru_conv_b, lru_wr, lru_br, lru_wi, lru_bi, lru_lambda, ssd_conv_w, ssd_conv_b, ssd_dt_bias, ssd_a_log, ssd_d, ssd_norm_g, ffn_w1, ffn_w3, ffn_w2, final_norm_g):
    depth, d_model, d_in = w_in.shape
    gw = d_model // N_GROUPS
    gwb = gw // LANE
    gp = gw // SSD_G
    assert gw // HG_H == LANE and SSD_N == LANE and gp % LANE == 0
    d_in_pad = -(-d_in // IN_PROJ_TN) * IN_PROJ_TN

    w_in_p = jnp.pad(w_in.astype(BF16), ((0, 0), (0, 0), (0, d_in_pad - d_in)))
    w_out_b = w_out.astype(BF16)
    w1_b, w3_b, w2_b = ffn_w1.astype(BF16), ffn_w3.astype(BF16), ffn_w2.astype(BF16)
    row3 = lambda a: a.reshape(depth, 1, -1)
    ws_b = gmlp_ws.astype(BF16)
    bs_full = jnp.broadcast_to(gmlp_bs[..., None], gmlp_bs.shape + (gw // A_H,))
    lbs = jax.nn.softmax(hgrn_lb.astype(F32), axis=0)
    lb = row3(jnp.cumsum(lbs, axis=0) - lbs[0])
    m_tab = tuple(jnp.asarray(t, F32) for t in _hgrn_masks())
    wr_bd, wi_bd = _block_diag_pairs(lru_wr), _block_diag_pairs(lru_wi)
    wcat = jnp.concatenate([wr_bd[:, 0], wi_bd[:, 0], wr_bd[:, 1], wi_bd[:, 1]], axis=-1).astype(BF16)
    grp = lambda a: a.reshape(depth, 2, gwb, LANE)
    bcat = jnp.concatenate([grp(lru_br)[:, 0], grp(lru_bi)[:, 0], grp(lru_br)[:, 1], grp(lru_bi)[:, 1]],
                           axis=-1).reshape(depth, gwb, 1, 4 * LANE)
    dtb_e, alog_e = _head_lanes(ssd_dt_bias), _head_lanes(ssd_a_log)
    dd_e = row3(jnp.repeat(ssd_d, gw // SSD_H, axis=-1))
    fg = final_norm_g.reshape(1, d_model)

    n_dec = c.shape[0]
    rows = -(-(n_dec + 1) // SUBLANE) * SUBLANE
    cond = jnp.zeros((rows, d_model), F32).at[:n_dec].set(c).at[n_dec].set(c_ctx)
    mod = _modulation(cond, w_mod, b_mod)

    def run_pass(x, row_lo, row_hi, width, st_hg, st_lru, st_ssd):
        b, seq_len, _ = x.shape
        x2d = x.reshape(b * seq_len, d_model)
        s_hg, s_lru, s_ssd = [], [], []
        for l in range(depth):
            m = mod[l, row_lo:row_hi].reshape(row_hi - row_lo, 1, 6, d_model)
            sh1, sc1, g1, sh2, sc2, g2 = (m[:, :, k] for k in range(6))
            proj2d = _in_proj(x2d, row3(norm1_g), sh1, sc1, w_in_p, l, seq_len)
            proj3d = proj2d.reshape(b, seq_len, d_in_pad)
            ya = _gmlp(proj2d, row3(gmlp_norm_g), ws_b, bs_full, l, seq_len)
            yb, hg = _hgrn(proj3d, lb, row3(hgrn_norm_g), st_hg[l], m_tab, l, gwb)
            yc, lr = _lru(proj3d, lru_conv_w, row3(lru_conv_b), wcat, bcat, lru_lambda, st_lru[l], l, width, gwb)
            yd, sd = _ssd(proj3d, ssd_conv_w, row3(ssd_conv_b), dtb_e, alog_e,
                          dd_e, row3(ssd_norm_g), st_ssd[l], l, width, gwb)
            ys = [ya, yb.reshape(-1, gw), yc.reshape(-1, gw), yd.reshape(-1, gw)]
            x2d = _out_proj(x2d, ys, g1, w_out_b, l, seq_len)
            x2d = _ffn(x2d, row3(norm2_g), sh2, sc2, g2, w1_b, w3_b, w2_b, fg, l, seq_len, l == depth - 1)
            s_hg.append(hg)
            s_lru.append(lr)
            s_ssd.append(sd)
        return x2d.reshape(b, seq_len, d_model), s_hg, s_lru, s_ssd

    bp, seq, _ = x_prompt.shape
    hd = gw // HG_H
    z_hg = [jnp.zeros((bp, 2, HG_H, hd, hd), F32)] * depth
    z_lru = [jnp.zeros((bp, 2, gw), F32)] * depth
    z_ssd = [jnp.zeros((bp, 2, SSD_H, gw // SSD_H, SSD_N), F32)] * depth
    y_prompt, s_hg, s_lru, s_ssd = run_pass(x_prompt, n_dec, n_dec + 1, seq, z_hg, z_lru, z_ssd)

    st_hg = [state_hgrn[:, l] for l in range(depth)]
    st_lru = [state_rglru[:, l] for l in range(depth)]
    st_ssd = [state_ssd[:, l] for l in range(depth)]
    y_sample, _, _, _ = run_pass(x_sample, 0, n_dec, GRID_W, st_hg, st_lru, st_ssd)

    return (y_prompt, y_sample, jnp.stack(s_hg, axis=1), jnp.stack(s_lru, axis=1), jnp.stack(s_ssd, axis=1))
```

```python
import functools

import numpy as np
import jax
import jax.numpy as jnp
from jax import lax
from jax.experimental import pallas as pl
from jax.experimental.pallas import tpu as pltpu

F32 = jnp.float32
BF16 = jnp.bfloat16

EPS = 1e-6
N_GROUPS = 4
GRID_W = 64
A_CHUNK = 128
A_H = 4
HG_H = 4
LRU_H = 8
LRU_C = 8.0
CONV_K = 4
CONV_LEFT = 2
SSD_H = 8
SSD_G = 2
SSD_N = 128
LANE = 128
SUBLANE = 8
CHUNK = 128
MXU_N = 256
IN_PROJ_TN = 5 * MXU_N
NORM_ROWS = 256
VMEM_LIMIT = 56 * 1024 * 1024
NEG_BIG = -1e30


def _dot(a, b):
    return jnp.dot(a, b, preferred_element_type=F32)


def _dot_nt(a, b):
    return lax.dot_general(a, b, (((1,), (1,)), ((), ())), preferred_element_type=F32)


def _dot_tn(a, b):
    return lax.dot_general(a, b, (((0,), (0,)), ((), ())), preferred_element_type=F32)


def _split_terms(x, terms):
    hi = x.astype(BF16).astype(F32)
    r = x - hi
    if terms == 2:
        return [hi, r]
    mid = r.astype(BF16).astype(F32)
    return [hi, mid, r - mid]


def _dot01(m01, x, terms=3):
    return sum(_dot(m01, t.astype(BF16)) for t in _split_terms(x, terms))


def _sigmoid(x):
    return 0.5 * jnp.tanh(0.5 * x) + 0.5


def _sigmoid_rel(x):
    return jnp.exp(jnp.minimum(x, 0.0) - jnp.log(1.0 + jnp.exp(-jnp.abs(x))))


def _silu(x):
    return x * _sigmoid(x)


def _softplus(x):
    return jnp.maximum(x, 0.0) + jnp.log1p(jnp.exp(-jnp.abs(x)))


def _log_sigmoid(x):
    return jnp.minimum(x, 0.0) - jnp.log1p(jnp.exp(-jnp.abs(x)))


def _rms(x, g):
    return x * lax.rsqrt(jnp.mean(x * x, axis=-1, keepdims=True) + EPS) * g


def _conv_block(x, w, b, pos, width):
    n = x.shape[0]
    y = b + w[CONV_LEFT:CONV_LEFT + 1] * x
    for j in range(CONV_K):
        s = j - CONV_LEFT
        if s == 0:
            continue
        xs = pltpu.roll(x, (-s) % n, 0)
        ok = (pos + s >= 0) & (pos + s < width)
        y = y + w[j:j + 1] * jnp.where(ok, xs, 0.0)
    return y


def _cparams(*sem):
    return pltpu.CompilerParams(dimension_semantics=sem, vmem_limit_bytes=VMEM_LIMIT)


def _token_tile(t, seq_len, shared, cap):
    tm = min(cap, t if shared else seq_len)
    assert t % tm == 0 and (shared or seq_len % tm == 0)
    return tm


def _mod_kernel(c_ref, w_ref, b_ref, o_ref):
    s = _silu(c_ref[...]).astype(BF16)
    o_ref[...] = _dot(s, w_ref[...].astype(BF16)) + b_ref[...]


def _modulation(cond, w_mod, b_mod):
    depth, d, n = w_mod.shape
    rows = cond.shape[0]
    tn = 1024
    return pl.pallas_call(
        _mod_kernel,
        grid=(depth, n // tn),
        in_specs=[pl.BlockSpec((rows, d), lambda l, j: (0, 0)),
                  pl.BlockSpec((None, d, tn), lambda l, j: (l, 0, j)),
                  pl.BlockSpec((None, 1, tn), lambda l, j: (l, 0, j))],
        out_specs=pl.BlockSpec((None, rows, tn), lambda l, j: (l, 0, j)),
        out_shape=jax.ShapeDtypeStruct((depth, rows, n), F32),
        compiler_params=_cparams("arbitrary", "arbitrary"),
        name="modulation",
    )(cond, w_mod, b_mod.reshape(depth, 1, n))


def _norm_blocks(tm):
    rb = min(NORM_ROWS, tm)
    return [slice(r, r + rb) for r in range(0, tm, rb)]


def _in_proj_kernel(x_ref, g_ref, sh_ref, sc_ref, w_ref, o_ref, h_scr):
    j = pl.program_id(1)

    @pl.when(j == 0)
    def _():
        for rows in _norm_blocks(x_ref.shape[0]):
            h = (_rms(x_ref[rows, :], g_ref[...]) * (1.0 + sc_ref[...]) + sh_ref[...]).astype(BF16)
            h_scr[rows, :] = h
            o_ref[rows, :] = _dot(h, w_ref[...])

    @pl.when(j != 0)
    def _():
        o_ref[...] = _dot(h_scr[...], w_ref[...])


def _in_proj(x2d, g, sh, sc, w_in_p, layer, seq_len):
    t, d = x2d.shape
    n = w_in_p.shape[-1]
    shared = sh.shape[0] == 1
    tm = _token_tile(t, seq_len, shared, 1024)
    tn = IN_PROJ_TN
    per_seq = seq_len // tm
    mod_map = (lambda i, j: (0, 0, 0)) if shared else (lambda i, j: (i // per_seq, 0, 0))
    return pl.pallas_call(
        _in_proj_kernel,
        grid=(t // tm, n // tn),
        in_specs=[pl.BlockSpec((tm, d), lambda i, j: (i, 0)),
                  pl.BlockSpec((None, 1, d), lambda i, j: (layer, 0, 0)),
                  pl.BlockSpec((None, 1, d), mod_map),
                  pl.BlockSpec((None, 1, d), mod_map),
                  pl.BlockSpec((None, d, tn), lambda i, j: (layer, 0, j))],
        out_specs=pl.BlockSpec((tm, tn), lambda i, j: (i, j)),
        out_shape=jax.ShapeDtypeStruct((t, n), F32),
        scratch_shapes=[pltpu.VMEM((tm, d), BF16)],
        compiler_params=_cparams("arbitrary", "arbitrary"),
        name="in_proj",
    )(x2d, g, sh, sc, w_in_p)


def _gmlp_kernel(u_ref, v_ref, g_ref, ws_ref, bs_ref, o_ref):
    v = _rms(jax.nn.gelu(v_ref[...]), g_ref[...]).astype(BF16)
    u = jax.nn.gelu(u_ref[...])
    tm, gw = u.shape
    hd = gw // A_H
    for c in range(tm // A_CHUNK):
        rows = slice(c * A_CHUNK, (c + 1) * A_CHUNK)
        for h in range(A_H):
            cols = slice(h * hd, (h + 1) * hd)
            r = _dot(ws_ref[h], v[rows, cols]) + bs_ref[h]
            o_ref[rows, cols] = (u[rows, cols] * r).astype(BF16)


def _gmlp(proj2d, g, ws_b, bs_full, layer, seq_len):
    t = proj2d.shape[0]
    gw = g.shape[-1]
    tm = _token_tile(t, seq_len, True, 512)
    return pl.pallas_call(
        _gmlp_kernel,
        grid=(t // tm,),
        in_specs=[pl.BlockSpec((tm, gw), lambda i: (i, 0)),
                  pl.BlockSpec((tm, gw), lambda i: (i, 1)),
                  pl.BlockSpec((None, 1, gw), lambda i: (layer, 0, 0)),
                  pl.BlockSpec((None, A_H, A_CHUNK, A_CHUNK), lambda i: (layer, 0, 0, 0)),
                  pl.BlockSpec((None, A_H, A_CHUNK, gw // A_H), lambda i: (layer, 0, 0, 0))],
        out_specs=pl.BlockSpec((tm, gw), lambda i: (i, 0)),
        out_shape=jax.ShapeDtypeStruct((t, gw), BF16),
        compiler_params=_cparams("arbitrary"),
        name="gmlp",
    )(proj2d, proj2d, g, ws_b, bs_full)


HG_LEVELS = int(np.log2(CHUNK))


def _hgrn_masks():
    c = CHUNK
    mk = np.zeros((HG_LEVELS, c, c), np.float32)
    sg = np.zeros((HG_LEVELS, c, LANE), np.float32)
    u = np.arange(c)
    for lv in range(HG_LEVELS):
        blk = u // (c >> lv)
        later = (u % (c >> lv)) >= (c >> (lv + 1))
        mk[lv] = blk[:, None] == blk[None, :]
        sg[lv] = np.where(later, np.log2(np.e), -np.log2(np.e))[:, None]
    return mk, sg


def _cumsum_rows(x, reverse, row8):
    n = x.shape[0] // SUBLANE
    tiles = []
    for i in range(n):
        t = x[i * SUBLANE:(i + 1) * SUBLANE]
        d = 1
        while d < SUBLANE:
            shift = SUBLANE - d if reverse else d
            ok = (row8 < SUBLANE - d) if reverse else (row8 >= d)
            t = t + jnp.where(ok, pltpu.roll(t, shift, 0), 0.0)
            d *= 2
        tiles.append(t)
    carry = None
    for i in (range(n - 1, -1, -1) if reverse else range(n)):
        if carry is not None:
            tiles[i] = tiles[i] + carry
        carry = tiles[i][0:1] if reverse else tiles[i][SUBLANE - 1:SUBLANE]
    return jnp.concatenate(tiles, axis=0)


def _level_ref(cum, m, d, rowc, row8):
    c, hd = cum.shape
    off = m - 1 if d == 0 else m
    if m == 1:
        odd = (rowc % 2) == 1
        if d == 0:
            return jnp.where(odd, pltpu.roll(cum, 1, 0), cum)
        return jnp.where(odd, cum, pltpu.roll(cum, c - 1, 0))
    bcast = lambda r, n: jnp.broadcast_to(cum[r:r + 1, :], (n, hd))
    if 2 * m < SUBLANE:
        per = SUBLANE // (2 * m)
        pieces = []
        for v in range(c // SUBLANE):
            piece = bcast(v * SUBLANE + off, SUBLANE)
            for k in range(1, per):
                piece = jnp.where(row8 >= k * 2 * m, bcast(v * SUBLANE + k * 2 * m + off, SUBLANE), piece)
            pieces.append(piece)
        return jnp.concatenate(pieces, axis=0)
    return jnp.concatenate([bcast(k * 2 * m + off, 2 * m) for k in range(c // (2 * m))], axis=0)


def _hgrn_kernel(q_ref, ff_ref, fb_ref, i_ref, gt_ref, lb_ref, ng_ref, s0_ref, m_ref, sg_ref,
                 y_ref, sn_ref, o_scr, qd_scr, kv_scr, dec_scr, st_scr, *, seq_len):
    c = CHUNK
    n_chunks = seq_len // c
    hd = q_ref.shape[-1]
    lb = lb_ref[...]
    rowc = lax.broadcasted_iota(jnp.int32, (c, hd), 0)
    row8 = lax.broadcasted_iota(jnp.int32, (SUBLANE, hd), 0)
    f_refs = (ff_ref, fb_ref)

    def intra(i, carry):
        rows = pl.ds(pl.multiple_of(i * c, c), c)
        q = _silu(q_ref[rows, :])
        v = i_ref[rows, :]
        vb = v.astype(BF16)
        kk, cum, tot = [], [], []
        for d in range(2):
            f = lb + (1.0 - lb) * _sigmoid_rel(f_refs[d][rows, :])
            kk.append(1.0 - f)
            cum.append(_cumsum_rows(jnp.log(f), d == 1, row8))
            tot.append(cum[d][c - 1:c] if d == 0 else cum[d][0:1])
        scores = jnp.zeros((c, c), F32)
        for lv in range(HG_LEVELS):
            m = c >> (lv + 1)
            later = (rowc % (2 * m)) >= m
            sg = sg_ref[lv]
            el = [jnp.exp2((cum[0] - _level_ref(cum[0], m, 0, rowc, row8)) * sg),
                  jnp.exp2((_level_ref(cum[1], m, 1, rowc, row8) - cum[1]) * sg)]
            ql = jnp.concatenate([jnp.where(later, q * el[0], 0.0).astype(BF16),
                                  jnp.where(later, 0.0, q * el[1]).astype(BF16)], axis=1)
            kl = jnp.concatenate([jnp.where(later, 0.0, kk[0] * el[0]).astype(BF16),
                                  jnp.where(later, kk[1] * el[1], 0.0).astype(BF16)], axis=1)
            scores = scores + _dot_nt(ql, kl) * m_ref[lv]
        o_scr[rows, :] = (_dot(scores.astype(BF16), vb)
                          + jnp.sum(q * (kk[0] + kk[1]), axis=-1, keepdims=True) * v)
        kd = [(kk[d] * jnp.exp(tot[d] - cum[d])).astype(BF16) for d in range(2)]
        kv_scr[i] = _dot_tn(vb, jnp.concatenate(kd, axis=1))
        for d in range(2):
            qd_scr[d, rows, :] = (q * jnp.exp(cum[d])).astype(BF16)
            dec_scr[i, :, d * hd:(d + 1) * hd] = jnp.exp(tot[d])
        return carry

    lax.fori_loop(0, n_chunks, intra, 0, unroll=min(4, n_chunks))

    st_scr[0] = s0_ref[0].T
    st_scr[1] = s0_ref[1].T

    def inter(i, carry):
        for d in range(2):
            j = i if d == 0 else n_chunks - 1 - i
            rows = pl.ds(pl.multiple_of(j * c, c), c)
            st = st_scr[d]
            o_scr[rows, :] += _dot_nt(qd_scr[d, rows, :], st.astype(BF16))
            st_scr[d] = st * dec_scr[j, :, d * hd:(d + 1) * hd] + kv_scr[j, :, d * hd:(d + 1) * hd]
        return carry

    lax.fori_loop(0, n_chunks, inter, 0, unroll=min(4, n_chunks))
    sn_ref[0] = st_scr[0].T
    sn_ref[1] = st_scr[1].T

    def finish(i, carry):
        rows = pl.ds(pl.multiple_of(i * c, c), c)
        y_ref[rows, :] = (_rms(o_scr[rows, :], ng_ref[...]) * _silu(gt_ref[rows, :])).astype(BF16)
        return carry

    lax.fori_loop(0, n_chunks, finish, 0, unroll=min(4, n_chunks))


def _hgrn(proj3d, lb, ng, s0, m_tab, layer, gw_blocks):
    b, seq_len, _ = proj3d.shape
    hd = LANE
    n_chunks = seq_len // CHUNK
    base = 2 * gw_blocks
    col = lambda k: (lambda bi, h: (bi, 0, base + k * gw_blocks + h))
    specs = [pl.BlockSpec((None, seq_len, hd), col(k)) for k in range(5)]
    return pl.pallas_call(
        functools.partial(_hgrn_kernel, seq_len=seq_len),
        grid=(b, HG_H),
        in_specs=specs + [
            pl.BlockSpec((None, 1, hd), lambda bi, h: (layer, 0, h)),
            pl.BlockSpec((None, 1, hd), lambda bi, h: (layer, 0, 0)),
            pl.BlockSpec((None, 2, None, hd, hd), lambda bi, h: (bi, 0, h, 0, 0)),
            pl.BlockSpec((HG_LEVELS, CHUNK, CHUNK), lambda bi, h: (0, 0, 0)),
            pl.BlockSpec((HG_LEVELS, CHUNK, hd), lambda bi, h: (0, 0, 0))],
        out_specs=[pl.BlockSpec((None, seq_len, hd), lambda bi, h: (bi, 0, h)),
                   pl.BlockSpec((None, 2, None, hd, hd), lambda bi, h: (bi, 0, h, 0, 0))],
        out_shape=[jax.ShapeDtypeStruct((b, seq_len, HG_H * hd), BF16),
                   jax.ShapeDtypeStruct((b, 2, HG_H, hd, hd), F32)],
        scratch_shapes=[pltpu.VMEM((seq_len, hd), F32), pltpu.VMEM((2, seq_len, hd), BF16),
                        pltpu.VMEM((n_chunks, hd, 2 * hd), F32), pltpu.VMEM((n_chunks, 1, 2 * hd), F32),
                        pltpu.VMEM((2, hd, hd), F32)],
        compiler_params=_cparams("arbitrary", "arbitrary"),
        name="hgrn",
    )(proj3d, proj3d, proj3d, proj3d, proj3d, lb, ng, s0, *m_tab)


SCAN_ROWS = 64


def _scan_block(a_blk, b_blk, carry, row, reverse):
    n_tiles = a_blk.shape[0] // SUBLANE
    outs = [None] * n_tiles
    order = range(n_tiles - 1, -1, -1) if reverse else range(n_tiles)
    for i in order:
        a = a_blk[i * SUBLANE:(i + 1) * SUBLANE]
        bb = b_blk[i * SUBLANE:(i + 1) * SUBLANE]
        d = 1
        while d < SUBLANE:
            shift = SUBLANE - d if reverse else d
            ok = (row < SUBLANE - d) if reverse else (row >= d)
            a_s = jnp.where(ok, pltpu.roll(a, shift, 0), 1.0)
            b_s = jnp.where(ok, pltpu.roll(bb, shift, 0), 0.0)
            bb = a * b_s + bb
            a = a * a_s
            d *= 2
        h = bb + a * carry
        carry = h[0:1] if reverse else h[SUBLANE - 1:SUBLANE]
        outs[i] = h
    return jnp.concatenate(outs, axis=0), carry


def _prep_rows(seq_len, width):
    rows = max(width, min(256, seq_len))
    assert rows % width == 0 and seq_len % rows == 0
    return rows


def _lru_kernel(cx_ref, cg_ref, cw_ref, cb_ref, w_ref, bias_ref, lam_ref, h0_ref,
                y_ref, hn_ref, a_scr, b_scr, h_scr, *, seq_len, width):
    lanes = cx_ref.shape[-1]
    pr = _prep_rows(seq_len, width)
    pos = lax.broadcasted_iota(jnp.int32, (pr, lanes), 0) % width
    half_lsl = (0.5 * LRU_C) * _log_sigmoid(lam_ref[...])

    def prep(i, carry):
        rows = pl.ds(pl.multiple_of(i * pr, pr), pr)
        xc = _conv_block(cx_ref[rows, :], cw_ref[...], cb_ref[...], pos, width)
        half_xc = 0.5 * xc
        t = jnp.tanh(_dot(xc.astype(BF16), w_ref[...]) + bias_ref[...])
        for d in range(2):
            t_r = t[:, (2 * d) * lanes:(2 * d + 1) * lanes]
            t_i = t[:, (2 * d + 1) * lanes:(2 * d + 2) * lanes]
            hl = half_lsl[d:d + 1]
            log_a = t_r * hl + hl
            a = jnp.exp(log_a)
            w = -jnp.tanh(log_a) * (a * a + 1.0)
            root = jnp.exp(0.5 * jnp.log(w))
            bx = root * (t_i * half_xc + half_xc)
            a_scr[d, rows, :] = a
            b_scr[d, rows, :] = bx
        return carry

    lax.fori_loop(0, seq_len // pr, prep, 0, unroll=min(2, seq_len // pr))

    row = lax.broadcasted_iota(jnp.int32, (SUBLANE, lanes), 0)
    n_blocks = seq_len // SCAN_ROWS

    def scan(i, carry):
        cf, cb = carry
        rf = pl.ds(pl.multiple_of(i * SCAN_ROWS, SCAN_ROWS), SCAN_ROWS)
        rb = pl.ds(pl.multiple_of((n_blocks - 1 - i) * SCAN_ROWS, SCAN_ROWS), SCAN_ROWS)
        hf, cf = _scan_block(a_scr[0, rf, :], b_scr[0, rf, :], cf, row, False)
        hb, cb = _scan_block(a_scr[1, rb, :], b_scr[1, rb, :], cb, row, True)
        h_scr[0, rf, :] = hf
        h_scr[1, rb, :] = hb
        return cf, cb

    cf, cb = lax.fori_loop(0, n_blocks, scan, (h0_ref[0:1, :], h0_ref[1:2, :]))
    hn_ref[0:1, :] = cf
    hn_ref[1:2, :] = cb

    def finish(i, carry):
        rows = pl.ds(pl.multiple_of(i * pr, pr), pr)
        y_ref[rows, :] = ((h_scr[0, rows, :] + h_scr[1, rows, :]) * jax.nn.gelu(cg_ref[rows, :])).astype(BF16)
        return carry

    lax.fori_loop(0, seq_len // pr, finish, 0)


def _lru(proj3d, cw, cb, wcat, bcat, lam, h0, layer, width, gw_blocks):
    b, seq_len, _ = proj3d.shape
    n_grp = gw_blocks
    base = 7 * gw_blocks
    return pl.pallas_call(
        functools.partial(_lru_kernel, seq_len=seq_len, width=width),
        grid=(b, n_grp),
        in_specs=[pl.BlockSpec((None, seq_len, LANE), lambda bi, j: (bi, 0, base + j)),
                  pl.BlockSpec((None, seq_len, LANE), lambda bi, j: (bi, 0, base + gw_blocks + j)),
                  pl.BlockSpec((None, CONV_K, LANE), lambda bi, j: (layer, 0, j)),
                  pl.BlockSpec((None, 1, LANE), lambda bi, j: (layer, 0, j)),
                  pl.BlockSpec((None, None, LANE, 4 * LANE), lambda bi, j: (layer, j, 0, 0)),
                  pl.BlockSpec((None, None, 1, 4 * LANE), lambda bi, j: (layer, j, 0, 0)),
                  pl.BlockSpec((None, 2, LANE), lambda bi, j: (layer, 0, j)),
                  pl.BlockSpec((None, 2, LANE), lambda bi, j: (bi, 0, j))],
        out_specs=[pl.BlockSpec((None, seq_len, LANE), lambda bi, j: (bi, 0, j)),
                   pl.BlockSpec((None, 2, LANE), lambda bi, j: (bi, 0, j))],
        out_shape=[jax.ShapeDtypeStruct((b, seq_len, n_grp * LANE), BF16),
                   jax.ShapeDtypeStruct((b, 2, n_grp * LANE), F32)],
        scratch_shapes=[pltpu.VMEM((2, seq_len, LANE), F32), pltpu.VMEM((2, seq_len, LANE), F32),
                        pltpu.VMEM((2, seq_len, LANE), F32)],
        compiler_params=_cparams("arbitrary", "arbitrary"),
        name="rglru",
    )(proj3d, proj3d, cw, cb, wcat, bcat, lam, h0)


def _ssd_kernel(z_ref, x_ref, bm_ref, cm_ref, dt_ref,
                cwx_ref, cwb_ref, cwc_ref, cbx_ref, cbb_ref, cbc_ref,
                dtb_ref, alog_ref, dd_ref, ng_ref, h0_ref,
                y_ref, hn_ref, xs_scr, bs_scr, cs_scr, ya_scr, ecx_scr, inc_scr, dtot_scr, st_scr,
                *, seq_len, width):
    c = CHUNK
    n_chunks = seq_len // c
    gp = x_ref.shape[-1]
    n_hg = SSD_H // SSD_G
    hp = gp // n_hg
    pr = _prep_rows(seq_len, width)
    pos_x = lax.broadcasted_iota(jnp.int32, (pr, gp), 0) % width
    pos_n = lax.broadcasted_iota(jnp.int32, (pr, SSD_N), 0) % width

    def prep(i, carry):
        rows = pl.ds(pl.multiple_of(i * pr, pr), pr)
        xs_scr[rows, :] = _silu(_conv_block(x_ref[rows, :], cwx_ref[...], cbx_ref[...], pos_x, width))
        bs_scr[rows, :] = _silu(_conv_block(bm_ref[rows, :], cwb_ref[...], cbb_ref[...], pos_n, width))
        cs_scr[rows, :] = _silu(_conv_block(cm_ref[rows, :], cwc_ref[...], cbc_ref[...], pos_n, width))
        return carry

    lax.fori_loop(0, seq_len // pr, prep, 0)

    ri = lax.broadcasted_iota(jnp.int32, (c, c), 0)
    ci = lax.broadcasted_iota(jnp.int32, (c, c), 1)
    assert LANE == 2 * hp and SSD_G == 2
    lane_s = lax.broadcasted_iota(jnp.int32, (1, LANE), 1)
    lane_head = lax.broadcasted_iota(jnp.int32, (1, gp), 1) // hp

    grp = pl.program_id(1)
    row8 = lax.broadcasted_iota(jnp.int32, (SUBLANE, LANE), 0)
    causal = (ri >= ci, ri <= ci)
    first = lane_s < hp
    neg_a = -jnp.exp(alog_ref[...])

    def intra(i, carry):
        rows = pl.ds(pl.multiple_of(i * c, c), c)
        dt_c = _softplus(dt_ref[rows, :] + dtb_ref[...])
        a_c = neg_a * dt_c
        cum = (_cumsum_rows(a_c, False, row8), _cumsum_rows(a_c, True, row8))
        dt_t = dt_c.T
        cum_t = (cum[0].T, cum[1].T)
        xsb = xs_scr[rows, :].astype(BF16)
        bm_t = bs_scr[rows, :].T
        cm = cs_scr[rows, :].astype(BF16)
        g = _dot(cm, bm_t.astype(BF16))
        rhs = jnp.concatenate([jnp.where(lane_head == h, xsb, jnp.zeros_like(xsb)) for h in range(n_hg)], axis=0)
        s_parts, bw_parts = [], ([], [])
        for d in range(2):
            wides = []
            for h in range(n_hg):
                k = d * SSD_H + h
                pick = lambda x: jnp.where(grp == 0, x[k:k + 1], x[k + n_hg:k + n_hg + 1])
                col = jnp.broadcast_to(jnp.where(grp == 0, cum[d][:, k:k + 1], cum[d][:, k + n_hg:k + n_hg + 1]),
                                       (c, LANE))
                rw, dtr = pick(cum_t[d]), pick(dt_t)
                tot = rw[:, c - 1:c] if d == 0 else rw[:, 0:1]
                decay = jnp.exp(jnp.where(causal[d], col - rw, NEG_BIG))
                s_parts.append((g * decay * dtr).astype(BF16))
                bw_parts[d].append((bm_t * (dtr * jnp.exp(tot - rw))).astype(BF16))
                wides.append(col)
            cum_x = jnp.concatenate([jnp.where(first, wides[2 * j], wides[2 * j + 1]) for j in range(gp // LANE)],
                                    axis=1)
            ecx_scr[d, rows, :] = jnp.exp(cum_x)
            dtot_scr[i, d] = jnp.exp(cum_x[c - 1:c] if d == 0 else cum_x[0:1])
        ya_scr[rows, :] = _dot(jnp.concatenate(s_parts, axis=1), jnp.concatenate([rhs, rhs], axis=0))
        inc_scr[i] = _dot(jnp.concatenate([jnp.concatenate(bw_parts[0], axis=1),
                                           jnp.concatenate(bw_parts[1], axis=1)], axis=0), rhs)
        return carry

    lax.fori_loop(0, n_chunks, intra, 0, unroll=min(4, n_chunks))

    st_scr[0] = h0_ref[0].reshape(gp, SSD_N).T
    st_scr[1] = h0_ref[1].reshape(gp, SSD_N).T

    def inter(i, carry):
        for d in range(2):
            j = i if d == 0 else n_chunks - 1 - i
            rows = pl.ds(pl.multiple_of(j * c, c), c)
            st = st_scr[d]
            ya_scr[rows, :] += _dot(cs_scr[rows, :].astype(BF16), st.astype(BF16)) * ecx_scr[d, rows, :]
            st_scr[d] = st * dtot_scr[j, d] + inc_scr[j, d * SSD_N:(d + 1) * SSD_N, :]
        return carry

    lax.fori_loop(0, n_chunks, inter, 0, unroll=min(4, n_chunks))
    hn_ref[0] = st_scr[0].T.reshape(n_hg, hp, SSD_N)
    hn_ref[1] = st_scr[1].T.reshape(n_hg, hp, SSD_N)

    def finish(i, carry):
        rows = pl.ds(pl.multiple_of(i * c, c), c)
        y = ya_scr[rows, :] + dd_ref[...] * xs_scr[rows, :]
        y = y * _silu(z_ref[rows, :])
        y_ref[rows, :] = _rms(y, ng_ref[...]).astype(BF16)
        return carry

    lax.fori_loop(0, n_chunks, finish, 0, unroll=min(4, n_chunks))


def _ssd(proj3d, cw, cb, dtb, alog, dd, ng, h0, layer, width, gw_blocks):
    b, seq_len, _ = proj3d.shape
    gw = gw_blocks * LANE
    gp = gw // SSD_G
    n_hg = SSD_H // SSD_G
    hp = gp // n_hg
    xw = gp // LANE
    base_z = 9 * gw_blocks
    base_x = 10 * gw_blocks
    base_b = 11 * gw_blocks
    nb = SSD_N // LANE
    base_dt = base_b + 2 * SSD_G * nb
    return pl.pallas_call(
        functools.partial(_ssd_kernel, seq_len=seq_len, width=width),
        grid=(b, SSD_G),
        in_specs=[pl.BlockSpec((None, seq_len, gp), lambda bi, g: (bi, 0, base_z // xw + g)),
                  pl.BlockSpec((None, seq_len, gp), lambda bi, g: (bi, 0, base_x // xw + g)),
                  pl.BlockSpec((None, seq_len, SSD_N), lambda bi, g: (bi, 0, base_b + g)),
                  pl.BlockSpec((None, seq_len, SSD_N), lambda bi, g: (bi, 0, base_b + SSD_G * nb + g)),
                  pl.BlockSpec((None, seq_len, LANE), lambda bi, g: (bi, 0, base_dt)),
                  pl.BlockSpec((None, CONV_K, gp), lambda bi, g: (layer, 0, g)),
                  pl.BlockSpec((None, CONV_K, SSD_N), lambda bi, g: (layer, 0, gw // SSD_N + g)),
                  pl.BlockSpec((None, CONV_K, SSD_N), lambda bi, g: (layer, 0, gw // SSD_N + SSD_G + g)),
                  pl.BlockSpec((None, 1, gp), lambda bi, g: (layer, 0, g)),
                  pl.BlockSpec((None, 1, SSD_N), lambda bi, g: (layer, 0, gw // SSD_N + g)),
                  pl.BlockSpec((None, 1, SSD_N), lambda bi, g: (layer, 0, gw // SSD_N + SSD_G + g)),
                  pl.BlockSpec((None, 1, LANE), lambda bi, g: (layer, 0, 0)),
                  pl.BlockSpec((None, 1, LANE), lambda bi, g: (layer, 0, 0)),
                  pl.BlockSpec((None, 1, gp), lambda bi, g: (layer, 0, g)),
                  pl.BlockSpec((None, 1, gp), lambda bi, g: (layer, 0, g)),
                  pl.BlockSpec((None, 2, n_hg, hp, SSD_N), lambda bi, g: (bi, 0, g, 0, 0))],
        out_specs=[pl.BlockSpec((None, seq_len, gp), lambda bi, g: (bi, 0, g)),
                   pl.BlockSpec((None, 2, n_hg, hp, SSD_N), lambda bi, g: (bi, 0, g, 0, 0))],
        out_shape=[jax.ShapeDtypeStruct((b, seq_len, gw), BF16),
                   jax.ShapeDtypeStruct((b, 2, SSD_H, hp, SSD_N), F32)],
        scratch_shapes=[pltpu.VMEM((seq_len, gp), F32), pltpu.VMEM((seq_len, SSD_N), F32),
                        pltpu.VMEM((seq_len, SSD_N), F32), pltpu.VMEM((seq_len, gp), F32),
                        pltpu.VMEM((2, seq_len, gp), F32),
                        pltpu.VMEM((seq_len // CHUNK, 2 * SSD_N, gp), F32),
                        pltpu.VMEM((seq_len // CHUNK, 2, 1, gp), F32),
                        pltpu.VMEM((2, SSD_N, gp), F32)],
        compiler_params=_cparams("arbitrary", "arbitrary"),
        name="ssd",
    )(proj3d, proj3d, proj3d, proj3d, proj3d, cw, cw, cw, cb, cb, cb,
      dtb, alog, dd, ng, h0)


def _out_proj_kernel(x_ref, ya_ref, yb_ref, yc_ref, yd_ref, gate_ref, w_ref, o_ref):
    gw = ya_ref.shape[-1]
    acc = _dot(ya_ref[...], w_ref[0:gw, :])
    for k, r in enumerate((yb_ref, yc_ref, yd_ref), start=1):
        acc = acc + _dot(r[...], w_ref[k * gw:(k + 1) * gw, :])
    o_ref[...] = x_ref[...] + gate_ref[...] * acc


def _out_proj(x2d, ys, gate, w_out_b, layer, seq_len):
    t, d = x2d.shape
    gw = ys[0].shape[-1]
    shared = gate.shape[0] == 1
    tm = _token_tile(t, seq_len, shared, 512)
    per_seq = seq_len // tm
    mod_map = (lambda i: (0, 0, 0)) if shared else (lambda i: (i // per_seq, 0, 0))
    y_spec = pl.BlockSpec((tm, gw), lambda i: (i, 0))
    return pl.pallas_call(
        _out_proj_kernel,
        grid=(t // tm,),
        in_specs=[pl.BlockSpec((tm, d), lambda i: (i, 0)), y_spec, y_spec, y_spec, y_spec,
                  pl.BlockSpec((None, 1, d), mod_map),
                  pl.BlockSpec((None, N_GROUPS * gw, d), lambda i: (layer, 0, 0))],
        out_specs=pl.BlockSpec((tm, d), lambda i: (i, 0)),
        out_shape=jax.ShapeDtypeStruct((t, d), F32),
        compiler_params=_cparams("arbitrary"),
        name="out_proj",
    )(x2d, *ys, gate, w_out_b)


def _ffn_kernel(x_ref, g_ref, sh_ref, sc_ref, gate_ref, w1_ref, w3_ref, w2_ref, fg_ref, o_ref,
                h_scr, acc_scr, *, final_norm):
    j = pl.program_id(1)

    def partial_out(h):
        act = (_silu(_dot(h, w1_ref[...])) * _dot(h, w3_ref[...])).astype(BF16)
        return _dot(act, w2_ref[...])

    @pl.when(j == 0)
    def _():
        for rows in _norm_blocks(x_ref.shape[0]):
            h = (_rms(x_ref[rows, :], g_ref[...]) * (1.0 + sc_ref[...]) + sh_ref[...]).astype(BF16)
            h_scr[rows, :] = h
            acc_scr[rows, :] = partial_out(h)

    @pl.when(j != 0)
    def _():
        acc_scr[...] += partial_out(h_scr[...])

    @pl.when(j == pl.num_programs(1) - 1)
    def _():
        xo = x_ref[...] + gate_ref[...] * acc_scr[...]
        if final_norm:
            xo = _rms(xo, fg_ref[...])
        o_ref[...] = xo


def _ffn(x2d, g, sh, sc, gate, w1_b, w3_b, w2_b, fg, layer, seq_len, final_norm):
    t, d = x2d.shape
    dff = w1_b.shape[-1]
    shared = gate.shape[0] == 1
    tm = _token_tile(t, seq_len, shared, 512)
    tf = 512
    per_seq = seq_len // tm
    mod_map = (lambda i, j: (0, 0, 0)) if shared else (lambda i, j: (i // per_seq, 0, 0))
    return pl.pallas_call(
        functools.partial(_ffn_kernel, final_norm=final_norm),
        grid=(t // tm, dff // tf),
        in_specs=[pl.BlockSpec((tm, d), lambda i, j: (i, 0)),
                  pl.BlockSpec((None, 1, d), lambda i, j: (layer, 0, 0)),
                  pl.BlockSpec((None, 1, d), mod_map),
                  pl.BlockSpec((None, 1, d), mod_map),
                  pl.BlockSpec((None, 1, d), mod_map),
                  pl.BlockSpec((None, d, tf), lambda i, j: (layer, 0, j)),
                  pl.BlockSpec((None, d, tf), lambda i, j: (layer, 0, j)),
                  pl.BlockSpec((None, tf, d), lambda i, j: (layer, j, 0)),
                  pl.BlockSpec((1, d), lambda i, j: (0, 0))],
        out_specs=pl.BlockSpec((tm, d), lambda i, j: (i, 0)),
        out_shape=jax.ShapeDtypeStruct((t, d), F32),
        scratch_shapes=[pltpu.VMEM((tm, d), BF16), pltpu.VMEM((tm, d), F32)],
        compiler_params=_cparams("arbitrary", "arbitrary"),
        name="ffn",
    )(x2d, g, sh, sc, gate, w1_b, w3_b, w2_b, fg)


def _block_diag_pairs(w):
    depth, two, h, hd, _ = w.shape
    w = w.reshape(depth, two, h // 2, 2, hd, hd)
    z = jnp.zeros_like(w[:, :, :, 0])
    top = jnp.concatenate([w[:, :, :, 0], z], axis=-1)
    bot = jnp.concatenate([z, w[:, :, :, 1]], axis=-1)
    return jnp.concatenate([top, bot], axis=-2)


def _head_lanes(v):
    depth = v.shape[0]
    flat = v.reshape(depth, 1, 2 * SSD_H)
    return jnp.pad(flat, ((0, 0), (0, 0), (0, LANE - 2 * SSD_H)))


def kernel(x_prompt, x_sample, state_hgrn, state_rglru, state_ssd, c, c_ctx, w_mod, b_mod, norm1_g, norm2_g, w_in, w_out, gmlp_norm_g, gmlp_ws, gmlp_bs, hgrn_lb, hgrn_norm_g, lru_conv_w, lru_conv_b, lru_wr, lru_br, lru_wi, lru_bi, lru_lambda, ssd_conv_w, ssd_conv_b, ssd_dt_bias, ssd_a_log, ssd_d, ssd_norm_g, ffn_w1, ffn_w3, ffn_w2, final_norm_g):
    depth, d_model, d_in = w_in.shape
    gw = d_model // N_GROUPS
    gwb = gw // LANE
    gp = gw // SSD_G
    assert gw // HG_H == LANE and SSD_N == LANE and gp % LANE == 0
    d_in_pad = -(-d_in // IN_PROJ_TN) * IN_PROJ_TN

    w_in_p = jnp.pad(w_in.astype(BF16), ((0, 0), (0, 0), (0, d_in_pad - d_in)))
    w_out_b = w_out.astype(BF16)
    w1_b, w3_b, w2_b = ffn_w1.astype(BF16), ffn_w3.astype(BF16), ffn_w2.astype(BF16)
    row3 = lambda a: a.reshape(depth, 1, -1)
    ws_b = gmlp_ws.astype(BF16)
    bs_full = jnp.broadcast_to(gmlp_bs[..., None], gmlp_bs.shape + (gw // A_H,))
    lbs = jax.nn.softmax(hgrn_lb.astype(F32), axis=0)
    lb = row3(jnp.cumsum(lbs, axis=0) - lbs[0])
    m_tab = tuple(jnp.asarray(t, F32) for t in _hgrn_masks())
    wr_bd, wi_bd = _block_diag_pairs(lru_wr), _block_diag_pairs(lru_wi)
    wcat = (0.5 * jnp.concatenate([wr_bd[:, 0], wi_bd[:, 0], wr_bd[:, 1], wi_bd[:, 1]], axis=-1)).astype(BF16)
    grp = lambda a: a.reshape(depth, 2, gwb, LANE)
    bcat = 0.5 * jnp.concatenate([grp(lru_br)[:, 0], grp(lru_bi)[:, 0], grp(lru_br)[:, 1], grp(lru_bi)[:, 1]],
                                 axis=-1).reshape(depth, gwb, 1, 4 * LANE)
    dtb_e, alog_e = _head_lanes(ssd_dt_bias), _head_lanes(ssd_a_log)
    dd_e = row3(jnp.repeat(ssd_d, gw // SSD_H, axis=-1))
    fg = final_norm_g.reshape(1, d_model)

    n_dec = c.shape[0]
    rows = -(-(n_dec + 1) // SUBLANE) * SUBLANE
    cond = jnp.zeros((rows, d_model), F32).at[:n_dec].set(c).at[n_dec].set(c_ctx)
    mod = _modulation(cond, w_mod, b_mod)

    def run_pass(x, row_lo, row_hi, width, st_hg, st_lru, st_ssd):
        b, seq_len, _ = x.shape
        x2d = x.reshape(b * seq_len, d_model)
        s_hg, s_lru, s_ssd = [], [], []
        for l in range(depth):
            m = mod[l, row_lo:row_hi].reshape(row_hi - row_lo, 1, 6, d_model)
            sh1, sc1, g1, sh2, sc2, g2 = (m[:, :, k] for k in range(6))
            proj2d = _in_proj(x2d, row3(norm1_g), sh1, sc1, w_in_p, l, seq_len)
            proj3d = proj2d.reshape(b, seq_len, d_in_pad)
            ya = _gmlp(proj2d, row3(gmlp_norm_g), ws_b, bs_full, l, seq_len)
            yb, hg = _hgrn(proj3d, lb, row3(hgrn_norm_g), st_hg[l], m_tab, l, gwb)
            yc, lr = _lru(proj3d, lru_conv_w, row3(lru_conv_b), wcat, bcat, lru_lambda, st_lru[l], l, width, gwb)
            yd, sd = _ssd(proj3d, ssd_conv_w, row3(ssd_conv_b), dtb_e, alog_e,
                          dd_e, row3(ssd_norm_g), st_ssd[l], l, width, gwb)
            ys = [ya, yb.reshape(-1, gw), yc.reshape(-1, gw), yd.reshape(-1, gw)]
            x2d = _out_proj(x2d, ys, g1, w_out_b, l, seq_len)
            x2d = _ffn(x2d, row3(norm2_g), sh2, sc2, g2, w1_b, w3_b, w2_b, fg, l, seq_len, l == depth - 1)
            s_hg.append(hg)
            s_lru.append(lr)
            s_ssd.append(sd)
        return x2d.reshape(b, seq_len, d_model), s_hg, s_lru, s_ssd

    bp, seq, _ = x_prompt.shape
    hd = gw // HG_H
    z_hg = [jnp.zeros((bp, 2, HG_H, hd, hd), F32)] * depth
    z_lru = [jnp.zeros((bp, 2, gw), F32)] * depth
    z_ssd = [jnp.zeros((bp, 2, SSD_H, gw // SSD_H, SSD_N), F32)] * depth
    y_prompt, s_hg, s_lru, s_ssd = run_pass(x_prompt, n_dec, n_dec + 1, seq, z_hg, z_lru, z_ssd)

    st_hg = [state_hgrn[:, l] for l in range(depth)]
    st_lru = [state_rglru[:, l] for l in range(depth)]
    st_ssd = [state_ssd[:, l] for l in range(depth)]
    y_sample, _, _, _ = run_pass(x_sample, 0, n_dec, GRID_W, st_hg, st_lru, st_ssd)

    return (y_prompt, y_sample, jnp.stack(s_hg, axis=1), jnp.stack(s_lru, axis=1), jnp.stack(s_ssd, axis=1))
```

```python
import functools

import numpy as np
import jax
import jax.numpy as jnp
from jax import lax
from jax.experimental import pallas as pl
from jax.experimental.pallas import tpu as pltpu

F32 = jnp.float32
BF16 = jnp.bfloat16

EPS = 1e-6
N_GROUPS = 4
GRID_W = 64
A_CHUNK = 128
A_H = 4
HG_H = 4
LRU_C = 8.0
CONV_K = 4
CONV_LEFT = 2
SSD_H = 8
SSD_G = 2
SSD_N = 128
LANE = 128
SUBLANE = 8
CHUNK = 128
MXU_N = 256
IN_PROJ_TN = 5 * MXU_N
NORM_ROWS = 256
VMEM_LIMIT = 56 * 1024 * 1024
NEG_BIG = -1e30


def _dot(a, b):
    return jnp.dot(a, b, preferred_element_type=F32)


def _dot_nt(a, b):
    return lax.dot_general(a, b, (((1,), (1,)), ((), ())), preferred_element_type=F32)


def _dot_tn(a, b):
    return lax.dot_general(a, b, (((0,), (0,)), ((), ())), preferred_element_type=F32)


def _sigmoid(x):
    return 0.5 * jnp.tanh(0.5 * x) + 0.5


def _sigmoid_rel(x):
    return jnp.exp(jnp.minimum(x, 0.0) - jnp.log(1.0 + jnp.exp(-jnp.abs(x))))


def _silu(x):
    return x * _sigmoid(x)


def _softplus(x):
    return jnp.maximum(x, 0.0) + jnp.log1p(jnp.exp(-jnp.abs(x)))


def _log_sigmoid(x):
    return jnp.minimum(x, 0.0) - jnp.log1p(jnp.exp(-jnp.abs(x)))


def _rms(x, g):
    return x * lax.rsqrt(jnp.mean(x * x, axis=-1, keepdims=True) + EPS) * g


def _conv_block(x, w, b, pos, width):
    n = x.shape[0]
    y = b + w[CONV_LEFT:CONV_LEFT + 1] * x
    for j in range(CONV_K):
        s = j - CONV_LEFT
        if s == 0:
            continue
        xs = pltpu.roll(x, (-s) % n, 0)
        ok = (pos + s >= 0) & (pos + s < width)
        y = y + w[j:j + 1] * jnp.where(ok, xs, 0.0)
    return y


def _cparams(*sem):
    return pltpu.CompilerParams(dimension_semantics=sem, vmem_limit_bytes=VMEM_LIMIT)


def _token_tile(t, seq_len, shared, cap):
    tm = min(cap, t if shared else seq_len)
    assert t % tm == 0 and (shared or seq_len % tm == 0)
    return tm


def _mod_kernel(c_ref, w_ref, b_ref, o_ref):
    s = _silu(c_ref[...]).astype(BF16)
    o_ref[...] = _dot(s, w_ref[...].astype(BF16)) + b_ref[...]


def _modulation(cond, w_mod, b_mod):
    depth, d, n = w_mod.shape
    rows = cond.shape[0]
    tn = 1024
    return pl.pallas_call(
        _mod_kernel,
        grid=(depth, n // tn),
        in_specs=[pl.BlockSpec((rows, d), lambda l, j: (0, 0)),
                  pl.BlockSpec((None, d, tn), lambda l, j: (l, 0, j)),
                  pl.BlockSpec((None, 1, tn), lambda l, j: (l, 0, j))],
        out_specs=pl.BlockSpec((None, rows, tn), lambda l, j: (l, 0, j)),
        out_shape=jax.ShapeDtypeStruct((depth, rows, n), F32),
        compiler_params=_cparams("arbitrary", "arbitrary"),
        name="modulation",
    )(cond, w_mod, b_mod.reshape(depth, 1, n))


def _norm_blocks(tm):
    rb = min(NORM_ROWS, tm)
    return [slice(r, r + rb) for r in range(0, tm, rb)]


def _in_proj_kernel(x_ref, g_ref, sh_ref, sc_ref, w_ref, o_ref, h_scr):
    j = pl.program_id(1)

    @pl.when(j == 0)
    def _():
        for rows in _norm_blocks(x_ref.shape[0]):
            h = (_rms(x_ref[rows, :], g_ref[...]) * (1.0 + sc_ref[...]) + sh_ref[...]).astype(BF16)
            h_scr[rows, :] = h
            o_ref[rows, :] = _dot(h, w_ref[...])

    @pl.when(j != 0)
    def _():
        o_ref[...] = _dot(h_scr[...], w_ref[...])


def _in_proj(x2d, g, sh, sc, w_in_p, layer, seq_len):
    t, d = x2d.shape
    n = w_in_p.shape[-1]
    shared = sh.shape[0] == 1
    tm = _token_tile(t, seq_len, shared, 1024)
    tn = IN_PROJ_TN
    per_seq = seq_len // tm
    mod_map = (lambda i, j: (0, 0, 0)) if shared else (lambda i, j: (i // per_seq, 0, 0))
    return pl.pallas_call(
        _in_proj_kernel,
        grid=(t // tm, n // tn),
        in_specs=[pl.BlockSpec((tm, d), lambda i, j: (i, 0)),
                  pl.BlockSpec((None, 1, d), lambda i, j: (layer, 0, 0)),
                  pl.BlockSpec((None, 1, d), mod_map),
                  pl.BlockSpec((None, 1, d), mod_map),
                  pl.BlockSpec((None, d, tn), lambda i, j: (layer, 0, j))],
        out_specs=pl.BlockSpec((tm, tn), lambda i, j: (i, j)),
        out_shape=jax.ShapeDtypeStruct((t, n), F32),
        scratch_shapes=[pltpu.VMEM((tm, d), BF16)],
        compiler_params=_cparams("arbitrary", "arbitrary"),
        name="in_proj",
    )(x2d, g, sh, sc, w_in_p)


def _gmlp_kernel(u_ref, v_ref, g_ref, ws_ref, bs_ref, o_ref):
    v = _rms(jax.nn.gelu(v_ref[...]), g_ref[...]).astype(BF16)
    u = jax.nn.gelu(u_ref[...])
    tm, gw = u.shape
    hd = gw // A_H
    for c in range(tm // A_CHUNK):
        rows = slice(c * A_CHUNK, (c + 1) * A_CHUNK)
        for h in range(A_H):
            cols = slice(h * hd, (h + 1) * hd)
            r = _dot(ws_ref[h], v[rows, cols]) + bs_ref[h]
            o_ref[rows, cols] = (u[rows, cols] * r).astype(BF16)


def _gmlp(proj2d, g, ws_b, bs_full, layer, seq_len):
    t = proj2d.shape[0]
    gw = g.shape[-1]
    tm = _token_tile(t, seq_len, True, 512)
    return pl.pallas_call(
        _gmlp_kernel,
        grid=(t // tm,),
        in_specs=[pl.BlockSpec((tm, gw), lambda i: (i, 0)),
                  pl.BlockSpec((tm, gw), lambda i: (i, 1)),
                  pl.BlockSpec((None, 1, gw), lambda i: (layer, 0, 0)),
                  pl.BlockSpec((None, A_H, A_CHUNK, A_CHUNK), lambda i: (layer, 0, 0, 0)),
                  pl.BlockSpec((None, A_H, A_CHUNK, gw // A_H), lambda i: (layer, 0, 0, 0))],
        out_specs=pl.BlockSpec((tm, gw), lambda i: (i, 0)),
        out_shape=jax.ShapeDtypeStruct((t, gw), BF16),
        compiler_params=_cparams("arbitrary"),
        name="gmlp",
    )(proj2d, proj2d, g, ws_b, bs_full)


HG_LEVELS = int(np.log2(CHUNK))


def _hgrn_masks():
    c = CHUNK
    mk = np.zeros((HG_LEVELS, c, c), np.float32)
    sg = np.zeros((HG_LEVELS, c, LANE), np.float32)
    u = np.arange(c)
    for lv in range(HG_LEVELS):
        blk = u // (c >> lv)
        later = (u % (c >> lv)) >= (c >> (lv + 1))
        mk[lv] = blk[:, None] == blk[None, :]
        sg[lv] = np.where(later, np.log2(np.e), -np.log2(np.e))[:, None]
    return mk, sg


def _cumsum_rows(x, reverse, row8):
    n = x.shape[0] // SUBLANE
    tiles = []
    for i in range(n):
        t = x[i * SUBLANE:(i + 1) * SUBLANE]
        d = 1
        while d < SUBLANE:
            shift = SUBLANE - d if reverse else d
            ok = (row8 < SUBLANE - d) if reverse else (row8 >= d)
            t = t + jnp.where(ok, pltpu.roll(t, shift, 0), 0.0)
            d *= 2
        tiles.append(t)
    carry = None
    for i in (range(n - 1, -1, -1) if reverse else range(n)):
        if carry is not None:
            tiles[i] = tiles[i] + carry
        carry = tiles[i][0:1] if reverse else tiles[i][SUBLANE - 1:SUBLANE]
    return jnp.concatenate(tiles, axis=0)


def _level_ref(cum, m, d, rowc, row8):
    c, hd = cum.shape
    off = m - 1 if d == 0 else m
    if m == 1:
        odd = (rowc % 2) == 1
        if d == 0:
            return jnp.where(odd, pltpu.roll(cum, 1, 0), cum)
        return jnp.where(odd, cum, pltpu.roll(cum, c - 1, 0))
    bcast = lambda r, n: jnp.broadcast_to(cum[r:r + 1, :], (n, hd))
    if 2 * m < SUBLANE:
        per = SUBLANE // (2 * m)
        pieces = []
        for v in range(c // SUBLANE):
            piece = bcast(v * SUBLANE + off, SUBLANE)
            for k in range(1, per):
                piece = jnp.where(row8 >= k * 2 * m, bcast(v * SUBLANE + k * 2 * m + off, SUBLANE), piece)
            pieces.append(piece)
        return jnp.concatenate(pieces, axis=0)
    return jnp.concatenate([bcast(k * 2 * m + off, 2 * m) for k in range(c // (2 * m))], axis=0)


def _hgrn_kernel(q_ref, ff_ref, fb_ref, i_ref, gt_ref, lb_ref, ng_ref, s0_ref, m_ref, sg_ref,
                 y_ref, sn_ref, o_scr, qd_scr, kv_scr, dec_scr, st_scr, *, seq_len):
    c = CHUNK
    n_chunks = seq_len // c
    hd = q_ref.shape[-1]
    lb = lb_ref[...]
    rowc = lax.broadcasted_iota(jnp.int32, (c, hd), 0)
    row8 = lax.broadcasted_iota(jnp.int32, (SUBLANE, hd), 0)
    f_refs = (ff_ref, fb_ref)

    def intra(i, carry):
        rows = pl.ds(pl.multiple_of(i * c, c), c)
        q = _silu(q_ref[rows, :])
        v = i_ref[rows, :]
        vb = v.astype(BF16)
        kk, cum, tot = [], [], []
        for d in range(2):
            f = lb + (1.0 - lb) * _sigmoid_rel(f_refs[d][rows, :])
            kk.append(1.0 - f)
            cum.append(_cumsum_rows(jnp.log(f), d == 1, row8))
            tot.append(cum[d][c - 1:c] if d == 0 else cum[d][0:1])
        scores = None
        for lv in range(HG_LEVELS):
            m = c >> (lv + 1)
            later = (rowc % (2 * m)) >= m
            sg = sg_ref[lv]
            el = [jnp.exp2((cum[0] - _level_ref(cum[0], m, 0, rowc, row8)) * sg),
                  jnp.exp2((_level_ref(cum[1], m, 1, rowc, row8) - cum[1]) * sg)]
            ql = jnp.concatenate([jnp.where(later, q * el[0], 0.0).astype(BF16),
                                  jnp.where(later, 0.0, q * el[1]).astype(BF16)], axis=1)
            kl = jnp.concatenate([jnp.where(later, 0.0, kk[0] * el[0]).astype(BF16),
                                  jnp.where(later, kk[1] * el[1], 0.0).astype(BF16)], axis=1)
            scores = _dot_nt(ql, kl) if lv == 0 else scores + _dot_nt(ql, kl) * m_ref[lv]
        o_scr[rows, :] = (_dot(scores.astype(BF16), vb)
                          + jnp.sum(q * (kk[0] + kk[1]), axis=-1, keepdims=True) * v)
        kd = [(kk[d] * jnp.exp(tot[d] - cum[d])).astype(BF16) for d in range(2)]
        kv_scr[i] = _dot_tn(vb, jnp.concatenate(kd, axis=1))
        for d in range(2):
            qd_scr[d, rows, :] = (q * jnp.exp(cum[d])).astype(BF16)
            dec_scr[i, :, d * hd:(d + 1) * hd] = jnp.exp(tot[d])
        return carry

    lax.fori_loop(0, n_chunks, intra, 0, unroll=min(4, n_chunks))

    st_scr[0] = s0_ref[0].T
    st_scr[1] = s0_ref[1].T

    def inter(i, carry):
        for d in range(2):
            j = i if d == 0 else n_chunks - 1 - i
            rows = pl.ds(pl.multiple_of(j * c, c), c)
            st = st_scr[d]
            o_scr[rows, :] += _dot_nt(qd_scr[d, rows, :], st.astype(BF16))
            st_scr[d] = st * dec_scr[j, :, d * hd:(d + 1) * hd] + kv_scr[j, :, d * hd:(d + 1) * hd]
        return carry

    lax.fori_loop(0, n_chunks, inter, 0, unroll=min(4, n_chunks))
    sn_ref[0] = st_scr[0].T
    sn_ref[1] = st_scr[1].T

    def finish(i, carry):
        rows = pl.ds(pl.multiple_of(i * c, c), c)
        y_ref[rows, :] = (_rms(o_scr[rows, :], ng_ref[...]) * _silu(gt_ref[rows, :])).astype(BF16)
        return carry

    lax.fori_loop(0, n_chunks, finish, 0, unroll=min(4, n_chunks))


def _hgrn(proj3d, lb, ng, s0, m_tab, layer, gw_blocks):
    b, seq_len, _ = proj3d.shape
    hd = LANE
    n_chunks = seq_len // CHUNK
    base = 2 * gw_blocks
    col = lambda k: (lambda bi, h: (bi, 0, base + k * gw_blocks + h))
    specs = [pl.BlockSpec((None, seq_len, hd), col(k)) for k in range(5)]
    return pl.pallas_call(
        functools.partial(_hgrn_kernel, seq_len=seq_len),
        grid=(b, HG_H),
        in_specs=specs + [
            pl.BlockSpec((None, 1, hd), lambda bi, h: (layer, 0, h)),
            pl.BlockSpec((None, 1, hd), lambda bi, h: (layer, 0, 0)),
            pl.BlockSpec((None, 2, None, hd, hd), lambda bi, h: (bi, 0, h, 0, 0)),
            pl.BlockSpec((HG_LEVELS, CHUNK, CHUNK), lambda bi, h: (0, 0, 0)),
            pl.BlockSpec((HG_LEVELS, CHUNK, hd), lambda bi, h: (0, 0, 0))],
        out_specs=[pl.BlockSpec((None, seq_len, hd), lambda bi, h: (bi, 0, h)),
                   pl.BlockSpec((None, 2, None, hd, hd), lambda bi, h: (bi, 0, h, 0, 0))],
        out_shape=[jax.ShapeDtypeStruct((b, seq_len, HG_H * hd), BF16),
                   jax.ShapeDtypeStruct((b, 2, HG_H, hd, hd), F32)],
        scratch_shapes=[pltpu.VMEM((seq_len, hd), F32), pltpu.VMEM((2, seq_len, hd), BF16),
                        pltpu.VMEM((n_chunks, hd, 2 * hd), F32), pltpu.VMEM((n_chunks, 1, 2 * hd), F32),
                        pltpu.VMEM((2, hd, hd), F32)],
        compiler_params=_cparams("arbitrary", "arbitrary"),
        name="hgrn",
    )(proj3d, proj3d, proj3d, proj3d, proj3d, lb, ng, s0, *m_tab)


SCAN_ROWS = 64


def _scan_block(a_blk, b_blk, carry, row, reverse):
    n_tiles = a_blk.shape[0] // SUBLANE
    outs = [None] * n_tiles
    order = range(n_tiles - 1, -1, -1) if reverse else range(n_tiles)
    for i in order:
        a = a_blk[i * SUBLANE:(i + 1) * SUBLANE]
        bb = b_blk[i * SUBLANE:(i + 1) * SUBLANE]
        d = 1
        while d < SUBLANE:
            shift = SUBLANE - d if reverse else d
            ok = (row < SUBLANE - d) if reverse else (row >= d)
            a_s = jnp.where(ok, pltpu.roll(a, shift, 0), 1.0)
            b_s = jnp.where(ok, pltpu.roll(bb, shift, 0), 0.0)
            bb = a * b_s + bb
            a = a * a_s
            d *= 2
        h = bb + a * carry
        carry = h[0:1] if reverse else h[SUBLANE - 1:SUBLANE]
        outs[i] = h
    return jnp.concatenate(outs, axis=0), carry


def _prep_rows(seq_len, width):
    rows = max(width, min(256, seq_len))
    assert rows % width == 0 and seq_len % rows == 0
    return rows


def _lru_kernel(cx_ref, cg_ref, cw_ref, cb_ref, w_ref, bias_ref, lam_ref, h0_ref,
                y_ref, hn_ref, a_scr, b_scr, h_scr, *, seq_len, width):
    lanes = cx_ref.shape[-1]
    pr = _prep_rows(seq_len, width)
    pos = lax.broadcasted_iota(jnp.int32, (pr, lanes), 0) % width
    half_lsl = (0.5 * LRU_C) * _log_sigmoid(lam_ref[...])

    def prep(i, carry):
        rows = pl.ds(pl.multiple_of(i * pr, pr), pr)
        xc = _conv_block(cx_ref[rows, :], cw_ref[...], cb_ref[...], pos, width)
        half_xc = 0.5 * xc
        t = jnp.tanh(_dot(xc.astype(BF16), w_ref[...]) + bias_ref[...])
        for d in range(2):
            t_r = t[:, (2 * d) * lanes:(2 * d + 1) * lanes]
            t_i = t[:, (2 * d + 1) * lanes:(2 * d + 2) * lanes]
            hl = half_lsl[d:d + 1]
            log_a = t_r * hl + hl
            a = jnp.exp(log_a)
            w = -jnp.tanh(log_a) * (a * a + 1.0)
            root = jnp.exp(0.5 * jnp.log(w))
            bx = root * (t_i * half_xc + half_xc)
            a_scr[d, rows, :] = a
            b_scr[d, rows, :] = bx
        return carry

    lax.fori_loop(0, seq_len // pr, prep, 0, unroll=min(2, seq_len // pr))

    row = lax.broadcasted_iota(jnp.int32, (SUBLANE, lanes), 0)
    n_blocks = seq_len // SCAN_ROWS

    def scan(i, carry):
        cf, cb = carry
        rf = pl.ds(pl.multiple_of(i * SCAN_ROWS, SCAN_ROWS), SCAN_ROWS)
        rb = pl.ds(pl.multiple_of((n_blocks - 1 - i) * SCAN_ROWS, SCAN_ROWS), SCAN_ROWS)
        hf, cf = _scan_block(a_scr[0, rf, :], b_scr[0, rf, :], cf, row, False)
        hb, cb = _scan_block(a_scr[1, rb, :], b_scr[1, rb, :], cb, row, True)
        h_scr[0, rf, :] = hf
        h_scr[1, rb, :] = hb
        return cf, cb

    cf, cb = lax.fori_loop(0, n_blocks, scan, (h0_ref[0:1, :], h0_ref[1:2, :]))
    hn_ref[0:1, :] = cf
    hn_ref[1:2, :] = cb

    def finish(i, carry):
        rows = pl.ds(pl.multiple_of(i * pr, pr), pr)
        y_ref[rows, :] = ((h_scr[0, rows, :] + h_scr[1, rows, :]) * jax.nn.gelu(cg_ref[rows, :])).astype(BF16)
        return carry

    lax.fori_loop(0, seq_len // pr, finish, 0)


def _lru(proj3d, cw, cb, wcat, bcat, lam, h0, layer, width, gw_blocks):
    b, seq_len, _ = proj3d.shape
    n_grp = gw_blocks
    base = 7 * gw_blocks
    return pl.pallas_call(
        functools.partial(_lru_kernel, seq_len=seq_len, width=width),
        grid=(b, n_grp),
        in_specs=[pl.BlockSpec((None, seq_len, LANE), lambda bi, j: (bi, 0, base + j)),
                  pl.BlockSpec((None, seq_len, LANE), lambda bi, j: (bi, 0, base + gw_blocks + j)),
                  pl.BlockSpec((None, CONV_K, LANE), lambda bi, j: (layer, 0, j)),
                  pl.BlockSpec((None, 1, LANE), lambda bi, j: (layer, 0, j)),
                  pl.BlockSpec((None, None, LANE, 4 * LANE), lambda bi, j: (layer, j, 0, 0)),
                  pl.BlockSpec((None, None, 1, 4 * LANE), lambda bi, j: (layer, j, 0, 0)),
                  pl.BlockSpec((None, 2, LANE), lambda bi, j: (layer, 0, j)),
                  pl.BlockSpec((None, 2, LANE), lambda bi, j: (bi, 0, j))],
        out_specs=[pl.BlockSpec((None, seq_len, LANE), lambda bi, j: (bi, 0, j)),
                   pl.BlockSpec((None, 2, LANE), lambda bi, j: (bi, 0, j))],
        out_shape=[jax.ShapeDtypeStruct((b, seq_len, n_grp * LANE), BF16),
                   jax.ShapeDtypeStruct((b, 2, n_grp * LANE), F32)],
        scratch_shapes=[pltpu.VMEM((2, seq_len, LANE), F32), pltpu.VMEM((2, seq_len, LANE), F32),
                        pltpu.VMEM((2, seq_len, LANE), F32)],
        compiler_params=_cparams("arbitrary", "arbitrary"),
        name="rglru",
    )(proj3d, proj3d, cw, cb, wcat, bcat, lam, h0)


def _ssd_kernel(z_ref, x_ref, bm_ref, cm_ref, dt_ref,
                cwx_ref, cwb_ref, cwc_ref, cbx_ref, cbb_ref, cbc_ref,
                dtb_ref, alog_ref, dd_ref, ng_ref, h0_ref,
                y_ref, hn_ref, xs_scr, bs_scr, cs_scr, ya_scr, ecx_scr, inc_scr, dtot_scr, st_scr,
                *, seq_len, width):
    c = CHUNK
    n_chunks = seq_len // c
    gp = x_ref.shape[-1]
    n_hg = SSD_H // SSD_G
    hp = gp // n_hg
    pr = _prep_rows(seq_len, width)
    pos_x = lax.broadcasted_iota(jnp.int32, (pr, gp), 0) % width
    pos_n = lax.broadcasted_iota(jnp.int32, (pr, SSD_N), 0) % width

    def prep(i, carry):
        rows = pl.ds(pl.multiple_of(i * pr, pr), pr)
        xs_scr[rows, :] = _silu(_conv_block(x_ref[rows, :], cwx_ref[...], cbx_ref[...], pos_x, width))
        bs_scr[rows, :] = _silu(_conv_block(bm_ref[rows, :], cwb_ref[...], cbb_ref[...], pos_n, width))
        cs_scr[rows, :] = _silu(_conv_block(cm_ref[rows, :], cwc_ref[...], cbc_ref[...], pos_n, width))
        return carry

    lax.fori_loop(0, seq_len // pr, prep, 0)

    ri = lax.broadcasted_iota(jnp.int32, (c, c), 0)
    ci = lax.broadcasted_iota(jnp.int32, (c, c), 1)
    assert LANE == 2 * hp and SSD_G == 2
    lane_s = lax.broadcasted_iota(jnp.int32, (1, LANE), 1)
    lane_head = lax.broadcasted_iota(jnp.int32, (1, gp), 1) // hp

    grp = pl.program_id(1)
    row8 = lax.broadcasted_iota(jnp.int32, (SUBLANE, LANE), 0)
    causal = (ri >= ci, ri <= ci)
    first = lane_s < hp
    neg_a = -jnp.exp(alog_ref[...])

    def intra(i, carry):
        rows = pl.ds(pl.multiple_of(i * c, c), c)
        dt_c = _softplus(dt_ref[rows, :] + dtb_ref[...])
        a_c = neg_a * dt_c
        prefix = _cumsum_rows(a_c, False, row8)
        cum = (prefix, prefix[c - 1:c] - prefix + a_c)
        dt_t = dt_c.T
        cum_t = (cum[0].T, cum[1].T)
        xsb = xs_scr[rows, :].astype(BF16)
        bm_t = bs_scr[rows, :].T
        cm = cs_scr[rows, :].astype(BF16)
        g = _dot(cm, bm_t.astype(BF16))
        rhs = jnp.concatenate([jnp.where(lane_head == h, xsb, jnp.zeros_like(xsb)) for h in range(n_hg)], axis=0)
        s_parts, bw_parts = [], ([], [])
        for d in range(2):
            wides = []
            for h in range(n_hg):
                k = d * SSD_H + h
                pick = lambda x: jnp.where(grp == 0, x[k:k + 1], x[k + n_hg:k + n_hg + 1])
                col = jnp.broadcast_to(jnp.where(grp == 0, cum[d][:, k:k + 1], cum[d][:, k + n_hg:k + n_hg + 1]),
                                       (c, LANE))
                rw, dtr = pick(cum_t[d]), pick(dt_t)
                tot = rw[:, c - 1:c] if d == 0 else rw[:, 0:1]
                decay = jnp.exp(jnp.where(causal[d], col - rw, NEG_BIG))
                s_parts.append((g * decay * dtr).astype(BF16))
                bw_parts[d].append((bm_t * (dtr * jnp.exp(tot - rw))).astype(BF16))
                wides.append(col)
            cum_x = jnp.concatenate([jnp.where(first, wides[2 * j], wides[2 * j + 1]) for j in range(gp // LANE)],
                                    axis=1)
            ecx_scr[d, rows, :] = jnp.exp(cum_x)
            dtot_scr[i, d] = jnp.exp(cum_x[c - 1:c] if d == 0 else cum_x[0:1])
        ya_scr[rows, :] = _dot(jnp.concatenate(s_parts, axis=1), jnp.concatenate([rhs, rhs], axis=0))
        inc_scr[i] = _dot(jnp.concatenate([jnp.concatenate(bw_parts[0], axis=1),
                                           jnp.concatenate(bw_parts[1], axis=1)], axis=0), rhs)
        return carry

    lax.fori_loop(0, n_chunks, intra, 0, unroll=min(4, n_chunks))

    st_scr[0] = h0_ref[0].reshape(gp, SSD_N).T
    st_scr[1] = h0_ref[1].reshape(gp, SSD_N).T

    def inter(i, carry):
        for d in range(2):
            j = i if d == 0 else n_chunks - 1 - i
            rows = pl.ds(pl.multiple_of(j * c, c), c)
            st = st_scr[d]
            ya_scr[rows, :] += _dot(cs_scr[rows, :].astype(BF16), st.astype(BF16)) * ecx_scr[d, rows, :]
            st_scr[d] = st * dtot_scr[j, d] + inc_scr[j, d * SSD_N:(d + 1) * SSD_N, :]
        return carry

    lax.fori_loop(0, n_chunks, inter, 0, unroll=min(4, n_chunks))
    hn_ref[0] = st_scr[0].T.reshape(n_hg, hp, SSD_N)
    hn_ref[1] = st_scr[1].T.reshape(n_hg, hp, SSD_N)

    def finish(i, carry):
        rows = pl.ds(pl.multiple_of(i * c, c), c)
        y = ya_scr[rows, :] + dd_ref[...] * xs_scr[rows, :]
        y = y * _silu(z_ref[rows, :])
        y_ref[rows, :] = _rms(y, ng_ref[...]).astype(BF16)
        return carry

    lax.fori_loop(0, n_chunks, finish, 0, unroll=min(4, n_chunks))


def _ssd(proj3d, cw, cb, dtb, alog, dd, ng, h0, layer, width, gw_blocks):
    b, seq_len, _ = proj3d.shape
    gw = gw_blocks * LANE
    gp = gw // SSD_G
    n_hg = SSD_H // SSD_G
    hp = gp // n_hg
    xw = gp // LANE
    base_z = 9 * gw_blocks
    base_x = 10 * gw_blocks
    base_b = 11 * gw_blocks
    nb = SSD_N // LANE
    base_dt = base_b + 2 * SSD_G * nb
    return pl.pallas_call(
        functools.partial(_ssd_kernel, seq_len=seq_len, width=width),
        grid=(b, SSD_G),
        in_specs=[pl.BlockSpec((None, seq_len, gp), lambda bi, g: (bi, 0, base_z // xw + g)),
                  pl.BlockSpec((None, seq_len, gp), lambda bi, g: (bi, 0, base_x // xw + g)),
                  pl.BlockSpec((None, seq_len, SSD_N), lambda bi, g: (bi, 0, base_b + g)),
                  pl.BlockSpec((None, seq_len, SSD_N), lambda bi, g: (bi, 0, base_b + SSD_G * nb + g)),
                  pl.BlockSpec((None, seq_len, LANE), lambda bi, g: (bi, 0, base_dt)),
                  pl.BlockSpec((None, CONV_K, gp), lambda bi, g: (layer, 0, g)),
                  pl.BlockSpec((None, CONV_K, SSD_N), lambda bi, g: (layer, 0, gw // SSD_N + g)),
                  pl.BlockSpec((None, CONV_K, SSD_N), lambda bi, g: (layer, 0, gw // SSD_N + SSD_G + g)),
                  pl.BlockSpec((None, 1, gp), lambda bi, g: (layer, 0, g)),
                  pl.BlockSpec((None, 1, SSD_N), lambda bi, g: (layer, 0, gw // SSD_N + g)),
                  pl.BlockSpec((None, 1, SSD_N), lambda bi, g: (layer, 0, gw // SSD_N + SSD_G + g)),
                  pl.BlockSpec((None, 1, LANE), lambda bi, g: (layer, 0, 0)),
                  pl.BlockSpec((None, 1, LANE), lambda bi, g: (layer, 0, 0)),
                  pl.BlockSpec((None, 1, gp), lambda bi, g: (layer, 0, g)),
                  pl.BlockSpec((None, 1, gp), lambda bi, g: (layer, 0, g)),
                  pl.BlockSpec((None, 2, n_hg, hp, SSD_N), lambda bi, g: (bi, 0, g, 0, 0))],
        out_specs=[pl.BlockSpec((None, seq_len, gp), lambda bi, g: (bi, 0, g)),
                   pl.BlockSpec((None, 2, n_hg, hp, SSD_N), lambda bi, g: (bi, 0, g, 0, 0))],
        out_shape=[jax.ShapeDtypeStruct((b, seq_len, gw), BF16),
                   jax.ShapeDtypeStruct((b, 2, SSD_H, hp, SSD_N), F32)],
        scratch_shapes=[pltpu.VMEM((seq_len, gp), F32), pltpu.VMEM((seq_len, SSD_N), F32),
                        pltpu.VMEM((seq_len, SSD_N), F32), pltpu.VMEM((seq_len, gp), F32),
                        pltpu.VMEM((2, seq_len, gp), F32),
                        pltpu.VMEM((seq_len // CHUNK, 2 * SSD_N, gp), F32),
                        pltpu.VMEM((seq_len // CHUNK, 2, 1, gp), F32),
                        pltpu.VMEM((2, SSD_N, gp), F32)],
        compiler_params=_cparams("arbitrary", "arbitrary"),
        name="ssd",
    )(proj3d, proj3d, proj3d, proj3d, proj3d, cw, cw, cw, cb, cb, cb,
      dtb, alog, dd, ng, h0)


def _out_proj_kernel(x_ref, ya_ref, yb_ref, yc_ref, yd_ref, gate_ref, w_ref, o_ref):
    gw = ya_ref.shape[-1]
    acc = _dot(ya_ref[...], w_ref[0:gw, :])
    for k, r in enumerate((yb_ref, yc_ref, yd_ref), start=1):
        acc = acc + _dot(r[...], w_ref[k * gw:(k + 1) * gw, :])
    o_ref[...] = x_ref[...] + gate_ref[...] * acc


def _out_proj(x2d, ys, gate, w_out_b, layer, seq_len):
    t, d = x2d.shape
    gw = ys[0].shape[-1]
    shared = gate.shape[0] == 1
    tm = _token_tile(t, seq_len, shared, 512)
    per_seq = seq_len // tm
    mod_map = (lambda i: (0, 0, 0)) if shared else (lambda i: (i // per_seq, 0, 0))
    y_spec = pl.BlockSpec((tm, gw), lambda i: (i, 0))
    return pl.pallas_call(
        _out_proj_kernel,
        grid=(t // tm,),
        in_specs=[pl.BlockSpec((tm, d), lambda i: (i, 0)), y_spec, y_spec, y_spec, y_spec,
                  pl.BlockSpec((None, 1, d), mod_map),
                  pl.BlockSpec((None, N_GROUPS * gw, d), lambda i: (layer, 0, 0))],
        out_specs=pl.BlockSpec((tm, d), lambda i: (i, 0)),
        out_shape=jax.ShapeDtypeStruct((t, d), F32),
        compiler_params=_cparams("arbitrary"),
        name="out_proj",
    )(x2d, *ys, gate, w_out_b)


def _ffn_kernel(x_ref, g_ref, sh_ref, sc_ref, gate_ref, w1_ref, w3_ref, w2_ref, fg_ref, o_ref,
                h_scr, acc_scr, *, final_norm):
    j = pl.program_id(1)

    def partial_out(h):
        act = (_silu(_dot(h, w1_ref[...])) * _dot(h, w3_ref[...])).astype(BF16)
        return _dot(act, w2_ref[...])

    @pl.when(j == 0)
    def _():
        for rows in _norm_blocks(x_ref.shape[0]):
            h = (_rms(x_ref[rows, :], g_ref[...]) * (1.0 + sc_ref[...]) + sh_ref[...]).astype(BF16)
            h_scr[rows, :] = h
            acc_scr[rows, :] = partial_out(h)

    @pl.when(j != 0)
    def _():
        acc_scr[...] += partial_out(h_scr[...])

    @pl.when(j == pl.num_programs(1) - 1)
    def _():
        xo = x_ref[...] + gate_ref[...] * acc_scr[...]
        if final_norm:
            xo = _rms(xo, fg_ref[...])
        o_ref[...] = xo


def _ffn(x2d, g, sh, sc, gate, w1_b, w3_b, w2_b, fg, layer, seq_len, final_norm):
    t, d = x2d.shape
    dff = w1_b.shape[-1]
    shared = gate.shape[0] == 1
    tm = _token_tile(t, seq_len, shared, 512)
    tf = 512
    per_seq = seq_len // tm
    mod_map = (lambda i, j: (0, 0, 0)) if shared else (lambda i, j: (i // per_seq, 0, 0))
    return pl.pallas_call(
        functools.partial(_ffn_kernel, final_norm=final_norm),
        grid=(t // tm, dff // tf),
        in_specs=[pl.BlockSpec((tm, d), lambda i, j: (i, 0)),
                  pl.BlockSpec((None, 1, d), lambda i, j: (layer, 0, 0)),
                  pl.BlockSpec((None, 1, d), mod_map),
                  pl.BlockSpec((None, 1, d), mod_map),
                  pl.BlockSpec((None, 1, d), mod_map),
                  pl.BlockSpec((None, d, tf), lambda i, j: (layer, 0, j)),
                  pl.BlockSpec((None, d, tf), lambda i, j: (layer, 0, j)),
                  pl.BlockSpec((None, tf, d), lambda i, j: (layer, j, 0)),
                  pl.BlockSpec((1, d), lambda i, j: (0, 0))],
        out_specs=pl.BlockSpec((tm, d), lambda i, j: (i, 0)),
        out_shape=jax.ShapeDtypeStruct((t, d), F32),
        scratch_shapes=[pltpu.VMEM((tm, d), BF16), pltpu.VMEM((tm, d), F32)],
        compiler_params=_cparams("arbitrary", "arbitrary"),
        name="ffn",
    )(x2d, g, sh, sc, gate, w1_b, w3_b, w2_b, fg)


def _block_diag_pairs(w):
    depth, two, h, hd, _ = w.shape
    w = w.reshape(depth, two, h // 2, 2, hd, hd)
    z = jnp.zeros_like(w[:, :, :, 0])
    top = jnp.concatenate([w[:, :, :, 0], z], axis=-1)
    bot = jnp.concatenate([z, w[:, :, :, 1]], axis=-1)
    return jnp.concatenate([top, bot], axis=-2)


def _head_lanes(v):
    depth = v.shape[0]
    flat = v.reshape(depth, 1, 2 * SSD_H)
    return jnp.pad(flat, ((0, 0), (0, 0), (0, LANE - 2 * SSD_H)))


def kernel(x_prompt, x_sample, state_hgrn, state_rglru, state_ssd, c, c_ctx, w_mod, b_mod, norm1_g, norm2_g, w_in, w_out, gmlp_norm_g, gmlp_ws, gmlp_bs, hgrn_lb, hgrn_norm_g, lru_conv_w, lru_conv_b, lru_wr, lru_br, lru_wi, lru_bi, lru_lambda, ssd_conv_w, ssd_conv_b, ssd_dt_bias, ssd_a_log, ssd_d, ssd_norm_g, ffn_w1, ffn_w3, ffn_w2, final_norm_g):
    depth, d_model, d_in = w_in.shape
    gw = d_model // N_GROUPS
    gwb = gw // LANE
    gp = gw // SSD_G
    assert gw // HG_H == LANE and SSD_N == LANE and gp % LANE == 0
    d_in_pad = -(-d_in // IN_PROJ_TN) * IN_PROJ_TN

    w_in_p = jnp.pad(w_in.astype(BF16), ((0, 0), (0, 0), (0, d_in_pad - d_in)))
    w_out_b = w_out.astype(BF16)
    w1_b, w3_b, w2_b = ffn_w1.astype(BF16), ffn_w3.astype(BF16), ffn_w2.astype(BF16)
    row3 = lambda a: a.reshape(depth, 1, -1)
    ws_b = gmlp_ws.astype(BF16)
    bs_full = jnp.broadcast_to(gmlp_bs[..., None], gmlp_bs.shape + (gw // A_H,))
    lbs = jax.nn.softmax(hgrn_lb.astype(F32), axis=0)
    lb = row3(jnp.cumsum(lbs, axis=0) - lbs[0])
    m_tab = tuple(jnp.asarray(t, F32) for t in _hgrn_masks())
    wr_bd, wi_bd = _block_diag_pairs(lru_wr), _block_diag_pairs(lru_wi)
    wcat = (0.5 * jnp.concatenate([wr_bd[:, 0], wi_bd[:, 0], wr_bd[:, 1], wi_bd[:, 1]], axis=-1)).astype(BF16)
    grp = lambda a: a.reshape(depth, 2, gwb, LANE)
    bcat = 0.5 * jnp.concatenate([grp(lru_br)[:, 0], grp(lru_bi)[:, 0], grp(lru_br)[:, 1], grp(lru_bi)[:, 1]],
                                 axis=-1).reshape(depth, gwb, 1, 4 * LANE)
    dtb_e, alog_e = _head_lanes(ssd_dt_bias), _head_lanes(ssd_a_log)
    dd_e = row3(jnp.repeat(ssd_d, gw // SSD_H, axis=-1))
    fg = final_norm_g.reshape(1, d_model)

    n_dec = c.shape[0]
    rows = -(-(n_dec + 1) // SUBLANE) * SUBLANE
    cond = jnp.zeros((rows, d_model), F32).at[:n_dec].set(c).at[n_dec].set(c_ctx)
    mod = _modulation(cond, w_mod, b_mod)

    def run_pass(x, row_lo, row_hi, width, st_hg, st_lru, st_ssd):
        b, seq_len, _ = x.shape
        x2d = x.reshape(b * seq_len, d_model)
        s_hg, s_lru, s_ssd = [], [], []
        for l in range(depth):
            m = mod[l, row_lo:row_hi].reshape(row_hi - row_lo, 1, 6, d_model)
            sh1, sc1, g1, sh2, sc2, g2 = (m[:, :, k] for k in range(6))
            proj2d = _in_proj(x2d, row3(norm1_g), sh1, sc1, w_in_p, l, seq_len)
            proj3d = proj2d.reshape(b, seq_len, d_in_pad)
            ya = _gmlp(proj2d, row3(gmlp_norm_g), ws_b, bs_full, l, seq_len)
            yb, hg = _hgrn(proj3d, lb, row3(hgrn_norm_g), st_hg[l], m_tab, l, gwb)
            yc, lr = _lru(proj3d, lru_conv_w, row3(lru_conv_b), wcat, bcat, lru_lambda, st_lru[l], l, width, gwb)
            yd, sd = _ssd(proj3d, ssd_conv_w, row3(ssd_conv_b), dtb_e, alog_e,
                          dd_e, row3(ssd_norm_g), st_ssd[l], l, width, gwb)
            ys = [ya, yb.reshape(-1, gw), yc.reshape(-1, gw), yd.reshape(-1, gw)]
            x2d = _out_proj(x2d, ys, g1, w_out_b, l, seq_len)
            x2d = _ffn(x2d, row3(norm2_g), sh2, sc2, g2, w1_b, w3_b, w2_b, fg, l, seq_len, l == depth - 1)
            s_hg.append(hg)
            s_lru.append(lr)
            s_ssd.append(sd)
        return x2d.reshape(b, seq_len, d_model), s_hg, s_lru, s_ssd

    bp, seq, _ = x_prompt.shape
    hd = gw // HG_H
    z_hg = [jnp.zeros((bp, 2, HG_H, hd, hd), F32)] * depth
    z_lru = [jnp.zeros((bp, 2, gw), F32)] * depth
    z_ssd = [jnp.zeros((bp, 2, SSD_H, gw // SSD_H, SSD_N), F32)] * depth
    y_prompt, s_hg, s_lru, s_ssd = run_pass(x_prompt, n_dec, n_dec + 1, seq, z_hg, z_lru, z_ssd)

    st_hg = [state_hgrn[:, l] for l in range(depth)]
    st_lru = [state_rglru[:, l] for l in range(depth)]
    st_ssd = [state_ssd[:, l] for l in range(depth)]
    y_sample, _, _, _ = run_pass(x_sample, 0, n_dec, GRID_W, st_hg, st_lru, st_ssd)

    return (y_prompt, y_sample, jnp.stack(s_hg, axis=1), jnp.stack(s_lru, axis=1), jnp.stack(s_ssd, axis=1))
```
